```python
import jax, jax.numpy as jnp
from jax import lax
import numpy as np

D_MODEL = 1024
BATCH = 2
SEQ = 16384
DEPTH = 4

CHUNK = 64
N_MIXERS = 2
N_A = (DEPTH + 1) // 2
N_B = DEPTH // 2
CONV_WIDTH = 31
CONV_INNER = D_MODEL
POOL_INNER = D_MODEL
POOL_WINDOWS = (2, 4, 8, 16)
POOL_GROUPS = len(POOL_WINDOWS)
POOL_GC = POOL_INNER // POOL_GROUPS
RMS_EPS = 1e-6
LN_EPS = 1e-5

kernel_name = "hybrid_conformer_conv_multiscale_pool_trunk"


def rmsnorm(x, g):
    xf = x.astype(jnp.float32)
    y = xf * lax.rsqrt(jnp.mean(xf * xf, axis=-1, keepdims=True) + RMS_EPS)
    return (y * g.astype(jnp.float32)).astype(x.dtype)


def layernorm(x, g, b):
    xf = x.astype(jnp.float32)
    mu = jnp.mean(xf, axis=-1, keepdims=True)
    var = jnp.mean(jnp.square(xf - mu), axis=-1, keepdims=True)
    y = (xf - mu) * lax.rsqrt(var + LN_EPS)
    return (y * g.astype(jnp.float32) + b.astype(jnp.float32)).astype(x.dtype)


def conformer_conv_branch(h, w_in, dw, dw_b, ln_g, ln_b, w_out):
    p = jnp.einsum('bsd,de->bse', h, w_in)
    a, b, z = jnp.split(p, 3, axis=-1)
    u = a * jax.nn.sigmoid(b)
    u = lax.conv_general_dilated(
        u, dw[:, None, :].astype(u.dtype), window_strides=(1,),
        padding=[(CONV_WIDTH - 1, 0)],
        dimension_numbers=('NWC', 'WIO', 'NWC'),
        feature_group_count=CONV_INNER) + dw_b
    u = layernorm(u, ln_g, ln_b)
    u = jax.nn.silu(u) * jax.nn.silu(z)
    return jnp.einsum('bse,ed->bsd', u, w_out)


def multiscale_pool_branch(h, w_in, w_grp, b_grp, scale, w_out):
    p = jnp.einsum('bsd,de->bse', h, w_in)
    u, z = jnp.split(p, 2, axis=-1)
    S = u.shape[1]
    uf = u.astype(jnp.float32)
    cs = jnp.pad(jnp.cumsum(uf, axis=1), ((0, 0), (1, 0), (0, 0)))
    pos = jnp.arange(S, dtype=jnp.int32) + 1
    outs = []
    for g, w in enumerate(POOL_WINDOWS):
        sl = slice(g * POOL_GC, (g + 1) * POOL_GC)
        csg = cs[:, :, sl]
        upper = csg[:, 1:]
        lower = jnp.pad(csg[:, :S + 1 - w], ((0, 0), (w - 1, 0), (0, 0)))
        count = jnp.minimum(pos, w).astype(jnp.float32)[None, :, None]
        d = (upper - lower) / count - uf[:, :, sl]
        outs.append(jnp.einsum('bsc,cd->bsd', d.astype(u.dtype), w_grp[g]))
    y = (jnp.concatenate(outs, axis=-1) + b_grp) * scale
    y = y * jax.nn.silu(z)
    return jnp.einsum('bse,ed->bsd', y, w_out)


def setup_inputs(seed: int = 0) -> dict:
    key = jax.random.key(seed)
    ks = jax.random.split(key, 16)
    D, EA, EB = D_MODEL, CONV_INNER, POOL_INNER
    nrm = jax.random.normal
    f32 = jnp.float32
    return {
        "x": nrm(ks[0], (BATCH, SEQ, D), f32),
        "norm_g": 1.0 + 0.05 * nrm(ks[1], (DEPTH, D), f32),
        "final_g": 1.0 + 0.05 * nrm(ks[2], (D,), f32),
        "conv_w_in": nrm(ks[3], (N_A, D, 3 * EA), f32) * D ** -0.5,
        "conv_dw": nrm(ks[4], (N_A, CONV_WIDTH, EA), f32) * CONV_WIDTH ** -0.5,
        "conv_dw_b": 0.02 * nrm(ks[5], (N_A, EA), f32),
        "conv_ln_g": 1.0 + 0.05 * nrm(ks[6], (N_A, EA), f32),
        "conv_ln_b": 0.02 * nrm(ks[7], (N_A, EA), f32),
        "conv_w_out": nrm(ks[8], (N_A, EA, D), f32) * EA ** -0.5,
        "pool_w_in": nrm(ks[9], (N_B, D, 2 * EB), f32) * D ** -0.5,
        "pool_w_grp": nrm(ks[10], (N_B, POOL_GROUPS, POOL_GC, POOL_GC), f32) * POOL_GC ** -0.5,
        "pool_b_grp": 0.02 * nrm(ks[11], (N_B, EB), f32),
        "pool_scale": 1.0 + 0.1 * nrm(ks[12], (N_B, EB), f32),
        "pool_w_out": nrm(ks[13], (N_B, EB, D), f32) * EB ** -0.5,
    }


def reference(x, norm_g, final_g, conv_w_in, conv_dw, conv_dw_b, conv_ln_g,
              conv_ln_b, conv_w_out, pool_w_in, pool_w_grp, pool_b_grp,
              pool_scale, pool_w_out):
    h = x
    for i in range(DEPTH):
        hn = rmsnorm(h, norm_g[i])
        j = i // N_MIXERS
        if i % N_MIXERS == 0:
            y = conformer_conv_branch(hn, conv_w_in[j], conv_dw[j], conv_dw_b[j],
                                      conv_ln_g[j], conv_ln_b[j], conv_w_out[j])
        else:
            y = multiscale_pool_branch(hn, pool_w_in[j], pool_w_grp[j], pool_b_grp[j],
                                       pool_scale[j], pool_w_out[j])
        h = h + y
    return rmsnorm(h, final_g)
```

```python
import functools

import jax
import jax.numpy as jnp
from jax import lax
from jax.experimental import pallas as pl
from jax.experimental.pallas import tpu as pltpu

RMS_EPS = 1e-6
LN_EPS = 1e-5
CONV_WIDTH = 31
POOL_WINDOWS = (2, 4, 8, 16)

SUBLANES = 8
LANES = 128

TILE_T = 512
SEG = TILE_T // SUBLANES
CONV_HALO = SUBLANES * (CONV_WIDTH - 1)
POOL_HALO = SUBLANES * (max(POOL_WINDOWS) - 1)
ROW_CHUNK = 64
CONV_ROWS = 64
VMEM_LIMIT_BYTES = 56 * 1024 * 1024


def _sigmoid(x):
    return 1.0 / (1.0 + jnp.exp(-x))


def _rmsnorm_rows(x, g):
    ms = jnp.mean(x * x, axis=-1, keepdims=True)
    return x * lax.rsqrt(ms + RMS_EPS) * g


def _row_loop(n_rows, chunk, body):
    def step(i, carry):
        body(pl.multiple_of(i * chunk, chunk))
        return carry
    lax.fori_loop(0, n_rows // chunk, step, 0)


def _interleave(dst_ref, src_ref):
    D = src_ref.shape[1]
    for jb in range(SEG // SUBLANES):
        x = jnp.stack([src_ref[i * SEG + jb * SUBLANES:i * SEG + (jb + 1) * SUBLANES, :]
                       for i in range(SUBLANES)], axis=0)
        y = jnp.transpose(x, (1, 0, 2)).reshape(SUBLANES * SUBLANES, D)
        dst_ref[jb * SUBLANES * SUBLANES:(jb + 1) * SUBLANES * SUBLANES, :] = y


def _deinterleave(dst_ref, src_ref):
    D = src_ref.shape[1]
    for jb in range(SEG // SUBLANES):
        y = src_ref[jb * SUBLANES * SUBLANES:(jb + 1) * SUBLANES * SUBLANES, :]
        x = jnp.transpose(y.reshape(SUBLANES, SUBLANES, D), (1, 0, 2))
        for i in range(SUBLANES):
            dst_ref[i * SEG + jb * SUBLANES:i * SEG + (jb + 1) * SUBLANES, :] = x[i]


def _normalize_input(h_ref, g_ref, hn_scr):
    def norm_rows(r0):
        rows = pl.ds(r0, ROW_CHUNK)
        hn_scr[rows, :] = _rmsnorm_rows(h_ref[rows, :], g_ref[...]).astype(jnp.bfloat16)
    _row_loop(hn_scr.shape[0], ROW_CHUNK, norm_rows)


def _fill_history(u_scr, carry_scr, halo, T):
    E = u_scr.shape[1]
    sub = lax.broadcasted_iota(jnp.int32, (SUBLANES, E), 0)
    for jj in range(halo // SUBLANES):
        rows = slice(jj * SUBLANES, (jj + 1) * SUBLANES)
        tail = u_scr[T + jj * SUBLANES:T + (jj + 1) * SUBLANES, :]
        merged = jnp.where(sub == SUBLANES - 1, carry_scr[rows, :], tail)
        u_scr[rows, :] = pltpu.roll(merged, 1, axis=0)
        carry_scr[rows, :] = tail


def _conv_layer_kernel(h_ref, g_ref, win_ref, dw_ref, dwb_ref, lng_ref, lnb_ref,
                       wout_ref, o_ref, hp_scr, hn_scr, u_scr, carry_scr, z_scr, c_scr,
                       v_scr, y_scr, *, natural_in, natural_out):
    T = hn_scr.shape[0]
    E = wout_ref.shape[0]

    @pl.when(pl.program_id(1) == 0)
    def _():
        carry_scr[...] = jnp.zeros(carry_scr.shape, jnp.float32)

    if natural_in:
        _interleave(hp_scr, h_ref)
        h_src = hp_scr
    else:
        h_src = h_ref
    _normalize_input(h_src, g_ref, hn_scr)

    hn = hn_scr[...]
    a = jnp.dot(hn, win_ref[:, 0:E], preferred_element_type=jnp.float32)
    b = jnp.dot(hn, win_ref[:, E:2 * E], preferred_element_type=jnp.float32)
    u_scr[CONV_HALO:CONV_HALO + T, :] = a * _sigmoid(b)
    z = jnp.dot(hn, win_ref[:, 2 * E:3 * E], preferred_element_type=jnp.float32)
    z_scr[...] = z * _sigmoid(z)

    _fill_history(u_scr, carry_scr, CONV_HALO, T)

    n_out = CONV_ROWS // SUBLANES

    def conv_col(jc, carry):
        cols = pl.ds(pl.multiple_of(jc * LANES, LANES), LANES)
        w = dw_ref[:, cols]
        wb = [jnp.broadcast_to(w[k:k + 1, :], (SUBLANES, LANES)) for k in range(CONV_WIDTH)]
        bias = jnp.broadcast_to(dwb_ref[:, cols], (SUBLANES, LANES))

        def conv_rows(ir, carry2):
            r0 = pl.multiple_of(ir * CONV_ROWS, CONV_ROWS)
            acc = [bias] * n_out
            for m in range(n_out + CONV_WIDTH - 1):
                x = u_scr[pl.ds(r0 + m * SUBLANES, SUBLANES), cols]
                for o in range(n_out):
                    k = m - o
                    if 0 <= k < CONV_WIDTH:
                        acc[o] = acc[o] + x * wb[k]
            c_scr[pl.ds(r0, CONV_ROWS), cols] = jnp.concatenate(acc, axis=0)
            return carry2
        lax.fori_loop(0, T // CONV_ROWS, conv_rows, 0)
        return carry
    lax.fori_loop(0, E // LANES, conv_col, 0)

    def ln_rows(r0):
        rows = pl.ds(r0, ROW_CHUNK)
        c = c_scr[rows, :]
        mu = jnp.mean(c, axis=-1, keepdims=True)
        cc = c - mu
        var = jnp.mean(cc * cc, axis=-1, keepdims=True)
        y = cc * lax.rsqrt(var + LN_EPS) * lng_ref[...] + lnb_ref[...]
        v_scr[rows, :] = (y * _sigmoid(y) * z_scr[rows, :]).astype(jnp.bfloat16)
    _row_loop(T, ROW_CHUNK, ln_rows)

    y_scr[...] = jnp.dot(v_scr[...], wout_ref[...], preferred_element_type=jnp.float32)

    dst = y_scr if natural_out else o_ref

    def out_rows(r0):
        rows = pl.ds(r0, ROW_CHUNK)
        dst[rows, :] = h_src[rows, :] + y_scr[rows, :]
    _row_loop(T, ROW_CHUNK, out_rows)
    if natural_out:
        _deinterleave(o_ref, y_scr)


def _pool_layer_kernel(h_ref, g_ref, win_ref, wgrp_ref, bgrp_ref, scale_ref, wout_ref,
                       fg_ref, o_ref, hp_scr, hn_scr, u_scr, carry_scr, z_scr, d_scr,
                       v_scr, y_scr, *, natural_in, natural_out, final_norm):
    T = hn_scr.shape[0]
    E = wout_ref.shape[0]
    n_groups = len(POOL_WINDOWS)
    gc = E // n_groups

    @pl.when(pl.program_id(1) == 0)
    def _():
        carry_scr[...] = jnp.zeros(carry_scr.shape, jnp.float32)

    if natural_in:
        _interleave(hp_scr, h_ref)
        h_src = hp_scr
    else:
        h_src = h_ref
    _normalize_input(h_src, g_ref, hn_scr)

    hn = hn_scr[...]
    u_scr[POOL_HALO:POOL_HALO + T, :] = jnp.dot(hn, win_ref[:, 0:E], preferred_element_type=jnp.float32)
    z = jnp.dot(hn, win_ref[:, E:2 * E], preferred_element_type=jnp.float32)
    z_scr[...] = z * _sigmoid(z)

    _fill_history(u_scr, carry_scr, POOL_HALO, T)

    t0 = pl.program_id(1) * T

    def pool_rows(r0):
        row = r0 + lax.broadcasted_iota(jnp.int32, (ROW_CHUNK, 1), 0)
        pos1 = (t0 + (row % SUBLANES) * SEG + row // SUBLANES + 1).astype(jnp.float32)
        for g, w in enumerate(POOL_WINDOWS):
            cols = slice(g * gc, (g + 1) * gc)
            back = SUBLANES * (w - 1)
            p = u_scr[pl.ds(r0 + POOL_HALO - back, ROW_CHUNK + back), cols]
            cur = p[back:, :]
            span = 1
            while span < w:
                shift = SUBLANES * span
                p = p[shift:, :] + p[:-shift, :]
                span *= 2
            d = p / jnp.minimum(pos1, float(w)) - cur
            d_scr[pl.ds(r0, ROW_CHUNK), cols] = d.astype(jnp.bfloat16)
    _row_loop(T, ROW_CHUNK, pool_rows)

    for g in range(n_groups):
        cols = slice(g * gc, (g + 1) * gc)
        yg = jnp.dot(d_scr[:, cols], wgrp_ref[g], preferred_element_type=jnp.float32)
        yg = (yg + bgrp_ref[:, cols]) * scale_ref[:, cols] * z_scr[:, cols]
        v_scr[:, cols] = yg.astype(jnp.bfloat16)

    y_scr[...] = jnp.dot(v_scr[...], wout_ref[...], preferred_element_type=jnp.float32)

    dst = y_scr if natural_out else o_ref

    def out_rows(r0):
        rows = pl.ds(r0, ROW_CHUNK)
        out = h_src[rows, :] + y_scr[rows, :]
        if final_norm:
            out = _rmsnorm_rows(out, fg_ref[...])
        dst[rows, :] = out
    _row_loop(T, ROW_CHUNK, out_rows)
    if natural_out:
        _deinterleave(o_ref, y_scr)


def _full_spec(shape):
    return pl.BlockSpec(shape, lambda b, s: (0,) * len(shape))


def _tile_spec(d):
    return pl.BlockSpec((None, TILE_T, d), lambda b, s: (b, s, 0))


def _compiler_params():
    return pltpu.CompilerParams(
        dimension_semantics=("arbitrary", "arbitrary"),
        vmem_limit_bytes=VMEM_LIMIT_BYTES,
    )


def _check_tiling(S):
    assert S % TILE_T == 0 and TILE_T % ROW_CHUNK == 0 and TILE_T % CONV_ROWS == 0
    assert SEG >= CONV_WIDTH - 1 and SEG >= max(POOL_WINDOWS) - 1


def _conv_layer(h, g, w_in, dw, dw_b, ln_g, ln_b, w_out, natural_in, natural_out):
    B, S, D = h.shape
    E = w_out.shape[0]
    _check_tiling(S)
    f32, bf16 = jnp.float32, jnp.bfloat16
    return pl.pallas_call(
        functools.partial(_conv_layer_kernel, natural_in=natural_in, natural_out=natural_out),
        grid=(B, S // TILE_T),
        in_specs=[
            _tile_spec(D),
            _full_spec((1, D)),
            _full_spec((D, 3 * E)),
            _full_spec((CONV_WIDTH, E)),
            _full_spec((1, E)),
            _full_spec((1, E)),
            _full_spec((1, E)),
            _full_spec((E, D)),
        ],
        out_specs=_tile_spec(D),
        out_shape=jax.ShapeDtypeStruct((B, S, D), f32),
        scratch_shapes=[
            pltpu.VMEM((TILE_T if natural_in else SUBLANES, D), f32),
            pltpu.VMEM((TILE_T, D), bf16),
            pltpu.VMEM((CONV_HALO + TILE_T, E), f32),
            pltpu.VMEM((CONV_HALO, E), f32),
            pltpu.VMEM((TILE_T, E), f32),
            pltpu.VMEM((TILE_T, E), f32),
            pltpu.VMEM((TILE_T, E), bf16),
            pltpu.VMEM((TILE_T, D), f32),
        ],
        compiler_params=_compiler_params(),
        name="conv_layer_in" if natural_in else "conv_layer",
    )(h, g.reshape(1, D), w_in.astype(bf16), dw, dw_b.reshape(1, E),
      ln_g.reshape(1, E), ln_b.reshape(1, E), w_out.astype(bf16))


def _pool_layer(h, g, w_in, w_grp, b_grp, scale, w_out, final_g, natural_in,
                natural_out, final_norm):
    B, S, D = h.shape
    E = w_out.shape[0]
    n_groups, gc, _ = w_grp.shape
    _check_tiling(S)
    f32, bf16 = jnp.float32, jnp.bfloat16
    return pl.pallas_call(
        functools.partial(_pool_layer_kernel, natural_in=natural_in,
                          natural_out=natural_out, final_norm=final_norm),
        grid=(B, S // TILE_T),
        in_specs=[
            _tile_spec(D),
            _full_spec((1, D)),
            _full_spec((D, 2 * E)),
            _full_spec((n_groups, gc, gc)),
            _full_spec((1, E)),
            _full_spec((1, E)),
            _full_spec((E, D)),
            _full_spec((1, D)),
        ],
        out_specs=_tile_spec(D),
        out_shape=jax.ShapeDtypeStruct((B, S, D), f32),
        scratch_shapes=[
            pltpu.VMEM((TILE_T if natural_in else SUBLANES, D), f32),
            pltpu.VMEM((TILE_T, D), bf16),
            pltpu.VMEM((POOL_HALO + TILE_T, E), f32),
            pltpu.VMEM((POOL_HALO, E), f32),
            pltpu.VMEM((TILE_T, E), f32),
            pltpu.VMEM((TILE_T, E), bf16),
            pltpu.VMEM((TILE_T, E), bf16),
            pltpu.VMEM((TILE_T, D), f32),
        ],
        compiler_params=_compiler_params(),
        name="pool_layer_out" if natural_out else "pool_layer",
    )(h, g.reshape(1, D), w_in.astype(bf16), w_grp.astype(bf16),
      b_grp.reshape(1, E), scale.reshape(1, E), w_out.astype(bf16),
      final_g.reshape(1, D))


def kernel(x, norm_g, final_g, conv_w_in, conv_dw, conv_dw_b, conv_ln_g, conv_ln_b,
           conv_w_out, pool_w_in, pool_w_grp, pool_b_grp, pool_scale, pool_w_out):
    depth = norm_g.shape[0]
    h = x
    for i in range(depth):
        j = i // 2
        first, last = i == 0, i == depth - 1
        if i % 2 == 0:
            assert not last, "the final RMSNorm is fused into a pooling layer"
            h = _conv_layer(h, norm_g[i], conv_w_in[j], conv_dw[j], conv_dw_b[j],
                            conv_ln_g[j], conv_ln_b[j], conv_w_out[j],
                            natural_in=first, natural_out=last)
        else:
            h = _pool_layer(h, norm_g[i], pool_w_in[j], pool_w_grp[j], pool_b_grp[j],
                            pool_scale[j], pool_w_out[j], final_g,
                            natural_in=first, natural_out=last, final_norm=last)
    return h
```

```python
import functools

import jax
import jax.numpy as jnp
from jax import lax
from jax.experimental import pallas as pl
from jax.experimental.pallas import tpu as pltpu

RMS_EPS = 1e-6
LN_EPS = 1e-5
CONV_WIDTH = 31
POOL_WINDOWS = (2, 4, 8, 16)

SUBLANES = 8
LANES = 128

TILE_T = 512
SEG = TILE_T // SUBLANES
CONV_HALO = SUBLANES * (CONV_WIDTH - 1)
POOL_HALO = SUBLANES * (max(POOL_WINDOWS) - 1)
ROW_CHUNK = 64
CONV_ROWS = 64
POOL_ROWS = 128
COL_CHUNK = 256
VMEM_LIMIT_BYTES = 56 * 1024 * 1024


def _sigmoid(x):
    return 1.0 / (1.0 + jnp.exp(-x))


def _rmsnorm_rows(x, g):
    ms = jnp.mean(x * x, axis=-1, keepdims=True)
    return x * lax.rsqrt(ms + RMS_EPS) * g


def _dot(a, b):
    return jnp.dot(a, b, preferred_element_type=jnp.float32)


def _interleave(dst_ref, src_ref):
    D = src_ref.shape[1]
    for jb in range(SEG // SUBLANES):
        x = jnp.stack([src_ref[i * SEG + jb * SUBLANES:i * SEG + (jb + 1) * SUBLANES, :]
                       for i in range(SUBLANES)], axis=0)
        y = jnp.transpose(x, (1, 0, 2)).reshape(SUBLANES * SUBLANES, D)
        dst_ref[jb * SUBLANES * SUBLANES:(jb + 1) * SUBLANES * SUBLANES, :] = y


def _deinterleave(dst_ref, src_ref):
    D = src_ref.shape[1]
    for jb in range(SEG // SUBLANES):
        y = src_ref[jb * SUBLANES * SUBLANES:(jb + 1) * SUBLANES * SUBLANES, :]
        x = jnp.transpose(y.reshape(SUBLANES, SUBLANES, D), (1, 0, 2))
        for i in range(SUBLANES):
            dst_ref[i * SEG + jb * SUBLANES:i * SEG + (jb + 1) * SUBLANES, :] = x[i]


def _normalize_input(h_ref, g_ref, hn_scr):
    g = g_ref[...]
    for r0 in range(0, hn_scr.shape[0], ROW_CHUNK):
        rows = slice(r0, r0 + ROW_CHUNK)
        hn_scr[rows, :] = _rmsnorm_rows(h_ref[rows, :], g).astype(jnp.bfloat16)


def _fill_history(u_scr, carry_scr, halo, T, cols):
    width = cols.stop - cols.start
    sub = lax.broadcasted_iota(jnp.int32, (SUBLANES, width), 0)
    for jj in range(halo // SUBLANES):
        rows = slice(jj * SUBLANES, (jj + 1) * SUBLANES)
        tail = u_scr[T + jj * SUBLANES:T + (jj + 1) * SUBLANES, cols]
        merged = jnp.where(sub == SUBLANES - 1, carry_scr[rows, cols], tail)
        u_scr[rows, cols] = pltpu.roll(merged, 1, axis=0)
        carry_scr[rows, cols] = tail


def _residual_out(h_src, y_scr, o_ref, fg_ref, natural_out, final_norm):
    dst = y_scr if natural_out else o_ref
    for r0 in range(0, y_scr.shape[0], ROW_CHUNK):
        rows = slice(r0, r0 + ROW_CHUNK)
        out = h_src[rows, :] + y_scr[rows, :]
        if final_norm:
            out = _rmsnorm_rows(out, fg_ref[...])
        dst[rows, :] = out
    if natural_out:
        _deinterleave(o_ref, y_scr)


def _conv_layer_kernel(h_ref, g_ref, win_ref, dw_ref, dwb_ref, lng_ref, lnb_ref,
                       wout_ref, o_ref, hp_scr, hn_scr, u_scr, carry_scr, z_scr, c_scr,
                       v_scr, y_scr, *, natural_in, natural_out):
    T = hn_scr.shape[0]
    E = wout_ref.shape[0]

    @pl.when(pl.program_id(1) == 0)
    def _():
        carry_scr[...] = jnp.zeros(carry_scr.shape, jnp.float32)

    if natural_in:
        _interleave(hp_scr, h_ref)
        h_src = hp_scr
    else:
        h_src = h_ref
    _normalize_input(h_src, g_ref, hn_scr)

    n_out = CONV_ROWS // SUBLANES
    for c0 in range(0, E, COL_CHUNK):
        cols = slice(c0, c0 + COL_CHUNK)
        hn = hn_scr[...]
        a = _dot(hn, win_ref[:, c0:c0 + COL_CHUNK])
        b = _dot(hn, win_ref[:, E + c0:E + c0 + COL_CHUNK])
        u_scr[CONV_HALO:CONV_HALO + T, cols] = a * _sigmoid(b)
        z = _dot(hn, win_ref[:, 2 * E + c0:2 * E + c0 + COL_CHUNK])
        z_scr[:, cols] = z * _sigmoid(z)

        _fill_history(u_scr, carry_scr, CONV_HALO, T, cols)

        for l0 in range(c0, c0 + COL_CHUNK, LANES):
            lanes = slice(l0, l0 + LANES)
            w = dw_ref[:, lanes]
            wb = [jnp.broadcast_to(w[k:k + 1, :], (SUBLANES, LANES)) for k in range(CONV_WIDTH)]
            bias = jnp.broadcast_to(dwb_ref[:, lanes], (SUBLANES, LANES))
            for r0 in range(0, T, CONV_ROWS):
                acc = [bias] * n_out
                for m in range(n_out + CONV_WIDTH - 1):
                    x = u_scr[r0 + m * SUBLANES:r0 + (m + 1) * SUBLANES, lanes]
                    for o in range(n_out):
                        k = m - o
                        if 0 <= k < CONV_WIDTH:
                            acc[o] = acc[o] + x * wb[k]
                c_scr[r0:r0 + CONV_ROWS, lanes] = jnp.concatenate(acc, axis=0)

    lng = lng_ref[...]
    lnb = lnb_ref[...]
    for r0 in range(0, T, ROW_CHUNK):
        rows = slice(r0, r0 + ROW_CHUNK)
        c = c_scr[rows, :]
        mu = jnp.mean(c, axis=-1, keepdims=True)
        cc = c - mu
        var = jnp.mean(cc * cc, axis=-1, keepdims=True)
        y = cc * lax.rsqrt(var + LN_EPS) * lng + lnb
        v_scr[rows, :] = (y * _sigmoid(y) * z_scr[rows, :]).astype(jnp.bfloat16)

    half = T // 2
    for r0 in range(0, T, half):
        y_scr[r0:r0 + half, :] = _dot(v_scr[r0:r0 + half, :], wout_ref[...])
    _residual_out(h_src, y_scr, o_ref, None, natural_out, False)


def _pool_layer_kernel(h_ref, g_ref, win_ref, wgrp_ref, bgrp_ref, scale_ref, wout_ref,
                       fg_ref, o_ref, hp_scr, hn_scr, u_scr, carry_scr, d_scr, v_scr,
                       y_scr, *, natural_in, natural_out, final_norm):
    T = hn_scr.shape[0]
    E = wout_ref.shape[0]
    gc = E // len(POOL_WINDOWS)

    @pl.when(pl.program_id(1) == 0)
    def _():
        carry_scr[...] = jnp.zeros(carry_scr.shape, jnp.float32)

    if natural_in:
        _interleave(hp_scr, h_ref)
        h_src = hp_scr
    else:
        h_src = h_ref
    _normalize_input(h_src, g_ref, hn_scr)

    t0 = pl.program_id(1) * T
    for g, w in enumerate(POOL_WINDOWS):
        cols = slice(g * gc, (g + 1) * gc)
        hn = hn_scr[...]
        u_scr[POOL_HALO:POOL_HALO + T, cols] = _dot(hn, win_ref[:, g * gc:(g + 1) * gc])
        z = _dot(hn, win_ref[:, E + g * gc:E + (g + 1) * gc])
        sz = z * _sigmoid(z)

        _fill_history(u_scr, carry_scr, POOL_HALO, T, cols)

        back = SUBLANES * (w - 1)
        for r0 in range(0, T, POOL_ROWS):
            p = u_scr[r0 + POOL_HALO - back:r0 + POOL_HALO + POOL_ROWS, cols]
            cur = p[back:, :]
            span = 1
            while span < w:
                shift = SUBLANES * span
                p = p[shift:, :] + p[:-shift, :]
                span *= 2
            short = min(max(SUBLANES * w - r0, 0), POOL_ROWS)
            if short < POOL_ROWS:
                d = p[short:, :] * (1.0 / w) - cur[short:, :]
                d_scr[r0 + short:r0 + POOL_ROWS, cols] = d.astype(jnp.bfloat16)
            if short:
                row = r0 + lax.broadcasted_iota(jnp.int32, (short, gc), 0)
                pos1 = t0 + (row % SUBLANES) * SEG + row // SUBLANES + 1
                ds = p[:short, :] / jnp.minimum(pos1, w).astype(jnp.float32) - cur[:short, :]
                d_scr[r0:r0 + short, cols] = ds.astype(jnp.bfloat16)

        yg = _dot(d_scr[:, cols], wgrp_ref[g])
        yg = (yg + bgrp_ref[:, cols]) * scale_ref[:, cols] * sz
        v_scr[:, cols] = yg.astype(jnp.bfloat16)

    half = T // 2
    for r0 in range(0, T, half):
        y_scr[r0:r0 + half, :] = _dot(v_scr[r0:r0 + half, :], wout_ref[...])
    _residual_out(h_src, y_scr, o_ref, fg_ref, natural_out, final_norm)


def _full_spec(shape):
    return pl.BlockSpec(shape, lambda b, s: (0,) * len(shape))


def _tile_spec(d):
    return pl.BlockSpec((None, TILE_T, d), lambda b, s: (b, s, 0))


def _compiler_params():
    return pltpu.CompilerParams(
        dimension_semantics=("arbitrary", "arbitrary"),
        vmem_limit_bytes=VMEM_LIMIT_BYTES,
    )


def _check_tiling(S, E):
    assert S % TILE_T == 0 and TILE_T % ROW_CHUNK == 0
    assert TILE_T % CONV_ROWS == 0 and TILE_T % POOL_ROWS == 0 and E % COL_CHUNK == 0
    assert SEG >= CONV_WIDTH - 1 and SEG >= max(POOL_WINDOWS) - 1


def _conv_layer(h, g, w_in, dw, dw_b, ln_g, ln_b, w_out, natural_in, natural_out):
    B, S, D = h.shape
    E = w_out.shape[0]
    _check_tiling(S, E)
    f32, bf16 = jnp.float32, jnp.bfloat16
    return pl.pallas_call(
        functools.partial(_conv_layer_kernel, natural_in=natural_in, natural_out=natural_out),
        grid=(B, S // TILE_T),
        in_specs=[
            _tile_spec(D),
            _full_spec((1, D)),
            _full_spec((D, 3 * E)),
            _full_spec((CONV_WIDTH, E)),
            _full_spec((1, E)),
            _full_spec((1, E)),
            _full_spec((1, E)),
            _full_spec((E, D)),
        ],
        out_specs=_tile_spec(D),
        out_shape=jax.ShapeDtypeStruct((B, S, D), f32),
        scratch_shapes=[
            pltpu.VMEM((TILE_T if natural_in else SUBLANES, D), f32),
            pltpu.VMEM((TILE_T, D), bf16),
            pltpu.VMEM((CONV_HALO + TILE_T, E), f32),
            pltpu.VMEM((CONV_HALO, E), f32),
            pltpu.VMEM((TILE_T, E), f32),
            pltpu.VMEM((TILE_T, E), f32),
            pltpu.VMEM((TILE_T, E), bf16),
            pltpu.VMEM((TILE_T, D), f32),
        ],
        compiler_params=_compiler_params(),
        name="conv_layer_in" if natural_in else "conv_layer",
    )(h, g.reshape(1, D), w_in.astype(bf16), dw, dw_b.reshape(1, E),
      ln_g.reshape(1, E), ln_b.reshape(1, E), w_out.astype(bf16))


def _pool_layer(h, g, w_in, w_grp, b_grp, scale, w_out, final_g, natural_in,
                natural_out, final_norm):
    B, S, D = h.shape
    E = w_out.shape[0]
    n_groups, gc, _ = w_grp.shape
    assert n_groups == len(POOL_WINDOWS)
    _check_tiling(S, E)
    f32, bf16 = jnp.float32, jnp.bfloat16
    return pl.pallas_call(
        functools.partial(_pool_layer_kernel, natural_in=natural_in,
                          natural_out=natural_out, final_norm=final_norm),
        grid=(B, S // TILE_T),
        in_specs=[
            _tile_spec(D),
            _full_spec((1, D)),
            _full_spec((D, 2 * E)),
            _full_spec((n_groups, gc, gc)),
            _full_spec((1, E)),
            _full_spec((1, E)),
            _full_spec((E, D)),
            _full_spec((1, D)),
        ],
        out_specs=_tile_spec(D),
        out_shape=jax.ShapeDtypeStruct((B, S, D), f32),
        scratch_shapes=[
            pltpu.VMEM((TILE_T if natural_in else SUBLANES, D), f32),
            pltpu.VMEM((TILE_T, D), bf16),
            pltpu.VMEM((POOL_HALO + TILE_T, E), f32),
            pltpu.VMEM((POOL_HALO, E), f32),
            pltpu.VMEM((TILE_T, E), bf16),
            pltpu.VMEM((TILE_T, E), bf16),
            pltpu.VMEM((TILE_T, D), f32),
        ],
        compiler_params=_compiler_params(),
        name="pool_layer_out" if natural_out else "pool_layer",
    )(h, g.reshape(1, D), w_in.astype(bf16), w_grp.astype(bf16),
      b_grp.reshape(1, E), scale.reshape(1, E), w_out.astype(bf16),
      final_g.reshape(1, D))


def kernel(x, norm_g, final_g, conv_w_in, conv_dw, conv_dw_b, conv_ln_g, conv_ln_b,
           conv_w_out, pool_w_in, pool_w_grp, pool_b_grp, pool_scale, pool_w_out):
    depth = norm_g.shape[0]
    h = x
    for i in range(depth):
        j = i // 2
        first, last = i == 0, i == depth - 1
        if i % 2 == 0:
            assert not last, "the final RMSNorm is fused into a pooling layer"
            h = _conv_layer(h, norm_g[i], conv_w_in[j], conv_dw[j], conv_dw_b[j],
                            conv_ln_g[j], conv_ln_b[j], conv_w_out[j],
                            natural_in=first, natural_out=last)
        else:
            h = _pool_layer(h, norm_g[i], pool_w_in[j], pool_w_grp[j], pool_b_grp[j],
                            pool_scale[j], pool_w_out[j], final_g,
                            natural_in=first, natural_out=last, final_norm=last)
    return h
```

```python
import functools

import jax
import jax.numpy as jnp
from jax import lax
from jax.experimental import pallas as pl
from jax.experimental.pallas import tpu as pltpu

RMS_EPS = 1e-6
LN_EPS = 1e-5
CONV_WIDTH = 31
POOL_WINDOWS = (2, 4, 8, 16)

SUBLANES = 8
LANES = 128

TILE_T = 512
SEG = TILE_T // SUBLANES
CONV_HALO = SUBLANES * (CONV_WIDTH - 1)
POOL_HALO = SUBLANES * (max(POOL_WINDOWS) - 1)
ROW_CHUNK = 64
CONV_ROWS = 64
CONV_ACC_SPLIT = 4
POOL_ROWS = 128
COL_CHUNK = 256
OUT_ROWS = 256
VMEM_LIMIT_BYTES = 56 * 1024 * 1024


def _sigmoid(x):
    return 1.0 / (1.0 + jnp.exp(-x))


def _rmsnorm_rows(x, g):
    ms = jnp.mean(x * x, axis=-1, keepdims=True)
    return x * lax.rsqrt(ms + RMS_EPS) * g


def _dot(a, b):
    return jnp.dot(a, b, preferred_element_type=jnp.float32)


def _interleave(dst_ref, src_ref):
    D = src_ref.shape[1]
    for jb in range(SEG // SUBLANES):
        x = jnp.stack([src_ref[i * SEG + jb * SUBLANES:i * SEG + (jb + 1) * SUBLANES, :]
                       for i in range(SUBLANES)], axis=0)
        y = jnp.transpose(x, (1, 0, 2)).reshape(SUBLANES * SUBLANES, D)
        dst_ref[jb * SUBLANES * SUBLANES:(jb + 1) * SUBLANES * SUBLANES, :] = y


def _deinterleave(dst_ref, src_ref):
    D = src_ref.shape[1]
    for jb in range(SEG // SUBLANES):
        y = src_ref[jb * SUBLANES * SUBLANES:(jb + 1) * SUBLANES * SUBLANES, :]
        x = jnp.transpose(y.reshape(SUBLANES, SUBLANES, D), (1, 0, 2))
        for i in range(SUBLANES):
            dst_ref[i * SEG + jb * SUBLANES:i * SEG + (jb + 1) * SUBLANES, :] = x[i]


def _normalize_input(h_ref, g_ref, hn_scr):
    g = g_ref[...]
    for r0 in range(0, hn_scr.shape[0], ROW_CHUNK):
        rows = slice(r0, r0 + ROW_CHUNK)
        hn_scr[rows, :] = _rmsnorm_rows(h_ref[rows, :], g).astype(jnp.bfloat16)


def _fill_history(u_scr, carry_scr, halo, T, cols):
    width = cols.stop - cols.start
    sub = lax.broadcasted_iota(jnp.int32, (SUBLANES, width), 0)
    for jj in range(halo // SUBLANES):
        rows = slice(jj * SUBLANES, (jj + 1) * SUBLANES)
        tail = u_scr[T + jj * SUBLANES:T + (jj + 1) * SUBLANES, cols]
        merged = jnp.where(sub == SUBLANES - 1, carry_scr[rows, cols], tail)
        u_scr[rows, cols] = pltpu.roll(merged, 1, axis=0)
        carry_scr[rows, cols] = tail


def _conv_lane_tile(u_scr, dw_ref, dwb_ref, c_scr, lanes, T):
    n_out = CONV_ROWS // SUBLANES
    w = dw_ref[:, lanes]
    wb = [jnp.broadcast_to(w[k:k + 1, :], (SUBLANES, LANES)) for k in range(CONV_WIDTH)]
    bias = jnp.broadcast_to(dwb_ref[:, lanes], (SUBLANES, LANES))

    def row_block(i, carry):
        r0 = pl.multiple_of(i * CONV_ROWS, CONV_ROWS)
        acc = [[bias] + [None] * (CONV_ACC_SPLIT - 1) for _ in range(n_out)]
        for m in range(n_out + CONV_WIDTH - 1):
            x = u_scr[pl.ds(r0 + m * SUBLANES, SUBLANES), lanes]
            for o in range(n_out):
                k = m - o
                if 0 <= k < CONV_WIDTH:
                    p = x * wb[k]
                    part = acc[o][k % CONV_ACC_SPLIT]
                    acc[o][k % CONV_ACC_SPLIT] = p if part is None else part + p
        outs = [functools.reduce(lambda s, t: s + t, parts) for parts in acc]
        c_scr[pl.ds(r0, CONV_ROWS), lanes] = jnp.concatenate(outs, axis=0)
        return carry
    lax.fori_loop(0, T // CONV_ROWS, row_block, 0)


def _residual_out(h_src, y_scr, o_ref, fg_ref, natural_out, final_norm):
    dst = y_scr if natural_out else o_ref
    for r0 in range(0, y_scr.shape[0], ROW_CHUNK):
        rows = slice(r0, r0 + ROW_CHUNK)
        out = h_src[rows, :] + y_scr[rows, :]
        if final_norm:
            out = _rmsnorm_rows(out, fg_ref[...])
        dst[rows, :] = out
    if natural_out:
        _deinterleave(o_ref, y_scr)


def _conv_layer_kernel(h_ref, g_ref, win_ref, dw_ref, dwb_ref, lng_ref, lnb_ref,
                       wout_ref, o_ref, hp_scr, hn_scr, u_scr, carry_scr, z_scr, c_scr,
                       v_scr, y_scr, *, natural_in, natural_out):
    T = hn_scr.shape[0]
    E = wout_ref.shape[0]

    @pl.when(pl.program_id(1) == 0)
    def _():
        carry_scr[...] = jnp.zeros(carry_scr.shape, jnp.float32)

    if natural_in:
        _interleave(hp_scr, h_ref)
        h_src = hp_scr
    else:
        h_src = h_ref
    _normalize_input(h_src, g_ref, hn_scr)

    half = T // 2
    for r0 in range(0, T, half):
        hn = hn_scr[r0:r0 + half, :]
        for c0 in range(0, E, COL_CHUNK):
            cols = slice(c0, c0 + COL_CHUNK)
            a = _dot(hn, win_ref[:, c0:c0 + COL_CHUNK])
            b = _dot(hn, win_ref[:, E + c0:E + c0 + COL_CHUNK])
            u_scr[CONV_HALO + r0:CONV_HALO + r0 + half, cols] = a * _sigmoid(b)
            z = _dot(hn, win_ref[:, 2 * E + c0:2 * E + c0 + COL_CHUNK])
            z_scr[r0:r0 + half, cols] = z * _sigmoid(z)

    _fill_history(u_scr, carry_scr, CONV_HALO, T, slice(0, E))

    for l0 in range(0, E, LANES):
        _conv_lane_tile(u_scr, dw_ref, dwb_ref, c_scr, slice(l0, l0 + LANES), T)

    lng = lng_ref[...]
    lnb = lnb_ref[...]
    for h0 in range(0, T, OUT_ROWS):
        for r0 in range(h0, h0 + OUT_ROWS, ROW_CHUNK):
            rows = slice(r0, r0 + ROW_CHUNK)
            c = c_scr[rows, :]
            mu = jnp.mean(c, axis=-1, keepdims=True)
            cc = c - mu
            var = jnp.mean(cc * cc, axis=-1, keepdims=True)
            y = cc * lax.rsqrt(var + LN_EPS) * lng + lnb
            v_scr[rows, :] = (y * _sigmoid(y) * z_scr[rows, :]).astype(jnp.bfloat16)
        y_scr[h0:h0 + OUT_ROWS, :] = _dot(v_scr[h0:h0 + OUT_ROWS, :], wout_ref[...])
    _residual_out(h_src, y_scr, o_ref, None, natural_out, False)


def _pool_layer_kernel(h_ref, g_ref, win_ref, wgrp_ref, bgrp_ref, scale_ref, wout_ref,
                       fg_ref, o_ref, hp_scr, hn_scr, u_scr, carry_scr, d_scr, v_scr,
                       y_scr, *, natural_in, natural_out, final_norm):
    T = hn_scr.shape[0]
    E = wout_ref.shape[0]
    gc = E // len(POOL_WINDOWS)

    @pl.when(pl.program_id(1) == 0)
    def _():
        carry_scr[...] = jnp.zeros(carry_scr.shape, jnp.float32)

    if natural_in:
        _interleave(hp_scr, h_ref)
        h_src = hp_scr
    else:
        h_src = h_ref
    _normalize_input(h_src, g_ref, hn_scr)

    t0 = pl.program_id(1) * T
    for g, w in enumerate(POOL_WINDOWS):
        cols = slice(g * gc, (g + 1) * gc)
        hn = hn_scr[...]
        u_scr[POOL_HALO:POOL_HALO + T, cols] = _dot(hn, win_ref[:, g * gc:(g + 1) * gc])
        z = _dot(hn, win_ref[:, E + g * gc:E + (g + 1) * gc])
        sz = z * _sigmoid(z)

        _fill_history(u_scr, carry_scr, POOL_HALO, T, cols)

        back = SUBLANES * (w - 1)
        for r0 in range(0, T, POOL_ROWS):
            p = u_scr[r0 + POOL_HALO - back:r0 + POOL_HALO + POOL_ROWS, cols]
            cur = p[back:, :]
            span = 1
            while span < w:
                shift = SUBLANES * span
                p = p[shift:, :] + p[:-shift, :]
                span *= 2
            short = min(max(SUBLANES * w - r0, 0), POOL_ROWS)
            if short < POOL_ROWS:
                d = p[short:, :] * (1.0 / w) - cur[short:, :]
                d_scr[r0 + short:r0 + POOL_ROWS, cols] = d.astype(jnp.bfloat16)
            if short:
                row = r0 + lax.broadcasted_iota(jnp.int32, (short, gc), 0)
                pos1 = t0 + (row % SUBLANES) * SEG + row // SUBLANES + 1
                ds = p[:short, :] / jnp.minimum(pos1, w).astype(jnp.float32) - cur[:short, :]
                d_scr[r0:r0 + short, cols] = ds.astype(jnp.bfloat16)

        yg = _dot(d_scr[:, cols], wgrp_ref[g])
        yg = (yg + bgrp_ref[:, cols]) * scale_ref[:, cols] * sz
        v_scr[:, cols] = yg.astype(jnp.bfloat16)

    half = T // 2
    for r0 in range(0, T, half):
        y_scr[r0:r0 + half, :] = _dot(v_scr[r0:r0 + half, :], wout_ref[...])
    _residual_out(h_src, y_scr, o_ref, fg_ref, natural_out, final_norm)


def _full_spec(shape):
    return pl.BlockSpec(shape, lambda b, s: (0,) * len(shape))


def _tile_spec(d):
    return pl.BlockSpec((None, TILE_T, d), lambda b, s: (b, s, 0))


def _compiler_params():
    return pltpu.CompilerParams(
        dimension_semantics=("arbitrary", "arbitrary"),
        vmem_limit_bytes=VMEM_LIMIT_BYTES,
    )


def _check_tiling(S, E):
    assert S % TILE_T == 0 and TILE_T % ROW_CHUNK == 0
    assert TILE_T % CONV_ROWS == 0 and TILE_T % POOL_ROWS == 0 and E % COL_CHUNK == 0
    assert SEG >= CONV_WIDTH - 1 and SEG >= max(POOL_WINDOWS) - 1


def _conv_layer(h, g, w_in, dw, dw_b, ln_g, ln_b, w_out, natural_in, natural_out):
    B, S, D = h.shape
    E = w_out.shape[0]
    _check_tiling(S, E)
    f32, bf16 = jnp.float32, jnp.bfloat16
    return pl.pallas_call(
        functools.partial(_conv_layer_kernel, natural_in=natural_in, natural_out=natural_out),
        grid=(B, S // TILE_T),
        in_specs=[
            _tile_spec(D),
            _full_spec((1, D)),
            _full_spec((D, 3 * E)),
            _full_spec((CONV_WIDTH, E)),
            _full_spec((1, E)),
            _full_spec((1, E)),
            _full_spec((1, E)),
            _full_spec((E, D)),
        ],
        out_specs=_tile_spec(D),
        out_shape=jax.ShapeDtypeStruct((B, S, D), f32),
        scratch_shapes=[
            pltpu.VMEM((TILE_T if natural_in else SUBLANES, D), f32),
            pltpu.VMEM((TILE_T, D), bf16),
            pltpu.VMEM((CONV_HALO + TILE_T, E), f32),
            pltpu.VMEM((CONV_HALO, E), f32),
            pltpu.VMEM((TILE_T, E), f32),
            pltpu.VMEM((TILE_T, E), f32),
            pltpu.VMEM((TILE_T, E), bf16),
            pltpu.VMEM((TILE_T, D), f32),
        ],
        compiler_params=_compiler_params(),
        name="conv_layer_in" if natural_in else "conv_layer",
    )(h, g.reshape(1, D), w_in.astype(bf16), dw, dw_b.reshape(1, E),
      ln_g.reshape(1, E), ln_b.reshape(1, E), w_out.astype(bf16))


def _pool_layer(h, g, w_in, w_grp, b_grp, scale, w_out, final_g, natural_in,
                natural_out, final_norm):
    B, S, D = h.shape
    E = w_out.shape[0]
    n_groups, gc, _ = w_grp.shape
    assert n_groups == len(POOL_WINDOWS)
    _check_tiling(S, E)
    f32, bf16 = jnp.float32, jnp.bfloat16
    return pl.pallas_call(
        functools.partial(_pool_layer_kernel, natural_in=natural_in,
                          natural_out=natural_out, final_norm=final_norm),
        grid=(B, S // TILE_T),
        in_specs=[
            _tile_spec(D),
            _full_spec((1, D)),
            _full_spec((D, 2 * E)),
            _full_spec((n_groups, gc, gc)),
            _full_spec((1, E)),
            _full_spec((1, E)),
            _full_spec((E, D)),
            _full_spec((1, D)),
        ],
        out_specs=_tile_spec(D),
        out_shape=jax.ShapeDtypeStruct((B, S, D), f32),
        scratch_shapes=[
            pltpu.VMEM((TILE_T if natural_in else SUBLANES, D), f32),
            pltpu.VMEM((TILE_T, D), bf16),
            pltpu.VMEM((POOL_HALO + TILE_T, E), f32),
            pltpu.VMEM((POOL_HALO, E), f32),
            pltpu.VMEM((TILE_T, E), bf16),
            pltpu.VMEM((TILE_T, E), bf16),
            pltpu.VMEM((TILE_T, D), f32),
        ],
        compiler_params=_compiler_params(),
        name="pool_layer_out" if natural_out else "pool_layer",
    )(h, g.reshape(1, D), w_in.astype(bf16), w_grp.astype(bf16),
      b_grp.reshape(1, E), scale.reshape(1, E), w_out.astype(bf16),
      final_g.reshape(1, D))


def kernel(x, norm_g, final_g, conv_w_in, conv_dw, conv_dw_b, conv_ln_g, conv_ln_b,
           conv_w_out, pool_w_in, pool_w_grp, pool_b_grp, pool_scale, pool_w_out):
    depth = norm_g.shape[0]
    h = x
    for i in range(depth):
        j = i // 2
        first, last = i == 0, i == depth - 1
        if i % 2 == 0:
            assert not last, "the final RMSNorm is fused into a pooling layer"
            h = _conv_layer(h, norm_g[i], conv_w_in[j], conv_dw[j], conv_dw_b[j],
                            conv_ln_g[j], conv_ln_b[j], conv_w_out[j],
                            natural_in=first, natural_out=last)
        else:
            h = _pool_layer(h, norm_g[i], pool_w_in[j], pool_w_grp[j], pool_b_grp[j],
                            pool_scale[j], pool_w_out[j], final_g,
                            natural_in=first, natural_out=last, final_norm=last)
    return h
```

```python
import functools

import jax
import jax.numpy as jnp
from jax import lax
from jax.experimental import pallas as pl
from jax.experimental.pallas import tpu as pltpu

RMS_EPS = 1e-6
LN_EPS = 1e-5
CONV_WIDTH = 31
POOL_WINDOWS = (2, 4, 8, 16)

SUBLANES = 8
LANES = 128

TILE_T = 512
SEG = TILE_T // SUBLANES
CONV_HALO = SUBLANES * (CONV_WIDTH - 1)
POOL_HALO = SUBLANES * (max(POOL_WINDOWS) - 1)
ROW_CHUNK = 64
CONV_ROWS = 64
CONV_ACC_SPLIT = 4
POOL_ROWS = 128
COL_CHUNK = 256
OUT_ROWS = 256
VMEM_LIMIT_BYTES = 56 * 1024 * 1024


def _sigmoid(x):
    return 1.0 / (1.0 + jnp.exp(-x))


def _rmsnorm_rows(x, g):
    ms = jnp.mean(x * x, axis=-1, keepdims=True)
    return x * lax.rsqrt(ms + RMS_EPS) * g


def _dot(a, b):
    return jnp.dot(a, b, preferred_element_type=jnp.float32)


def _interleave(dst_ref, src_ref):
    D = src_ref.shape[1]
    for jb in range(SEG // SUBLANES):
        x = jnp.stack([src_ref[i * SEG + jb * SUBLANES:i * SEG + (jb + 1) * SUBLANES, :]
                       for i in range(SUBLANES)], axis=0)
        y = jnp.transpose(x, (1, 0, 2)).reshape(SUBLANES * SUBLANES, D)
        dst_ref[jb * SUBLANES * SUBLANES:(jb + 1) * SUBLANES * SUBLANES, :] = y


def _deinterleave(dst_ref, src_ref):
    D = src_ref.shape[1]
    for jb in range(SEG // SUBLANES):
        y = src_ref[jb * SUBLANES * SUBLANES:(jb + 1) * SUBLANES * SUBLANES, :]
        x = jnp.transpose(y.reshape(SUBLANES, SUBLANES, D), (1, 0, 2))
        for i in range(SUBLANES):
            dst_ref[i * SEG + jb * SUBLANES:i * SEG + (jb + 1) * SUBLANES, :] = x[i]


def _normalize_input(h_ref, g_ref, hn_scr):
    g = g_ref[...]
    for r0 in range(0, hn_scr.shape[0], ROW_CHUNK):
        rows = slice(r0, r0 + ROW_CHUNK)
        hn_scr[rows, :] = _rmsnorm_rows(h_ref[rows, :], g).astype(jnp.bfloat16)


def _fill_history(u_scr, carry_scr, halo, T, cols):
    width = cols.stop - cols.start
    sub = lax.broadcasted_iota(jnp.int32, (SUBLANES, width), 0)
    for jj in range(halo // SUBLANES):
        rows = slice(jj * SUBLANES, (jj + 1) * SUBLANES)
        tail = u_scr[T + jj * SUBLANES:T + (jj + 1) * SUBLANES, cols]
        merged = jnp.where(sub == SUBLANES - 1, carry_scr[rows, cols], tail)
        u_scr[rows, cols] = pltpu.roll(merged, 1, axis=0)
        carry_scr[rows, cols] = tail


def _conv_lane_tile(u_scr, dw_ref, dwb_ref, c_scr, lanes, T):
    n_out = CONV_ROWS // SUBLANES
    w = dw_ref[:, lanes]
    wb = [jnp.broadcast_to(w[k:k + 1, :], (SUBLANES, LANES)) for k in range(CONV_WIDTH)]
    bias = jnp.broadcast_to(dwb_ref[:, lanes], (SUBLANES, LANES))

    def row_block(i, carry):
        r0 = pl.multiple_of(i * CONV_ROWS, CONV_ROWS)
        acc = [[bias] + [None] * (CONV_ACC_SPLIT - 1) for _ in range(n_out)]
        for m in range(n_out + CONV_WIDTH - 1):
            x = u_scr[pl.ds(r0 + m * SUBLANES, SUBLANES), lanes]
            for o in range(n_out):
                k = m - o
                if 0 <= k < CONV_WIDTH:
                    p = x * wb[k]
                    part = acc[o][k % CONV_ACC_SPLIT]
                    acc[o][k % CONV_ACC_SPLIT] = p if part is None else part + p
        outs = [functools.reduce(lambda s, t: s + t, parts) for parts in acc]
        c_scr[pl.ds(r0, CONV_ROWS), lanes] = jnp.concatenate(outs, axis=0)
        return carry
    lax.fori_loop(0, T // CONV_ROWS, row_block, 0)


def _residual_out(h_src, y_scr, o_ref, fg_ref, natural_out, final_norm):
    dst = y_scr if natural_out else o_ref
    for r0 in range(0, y_scr.shape[0], ROW_CHUNK):
        rows = slice(r0, r0 + ROW_CHUNK)
        out = h_src[rows, :] + y_scr[rows, :]
        if final_norm:
            out = _rmsnorm_rows(out, fg_ref[...])
        dst[rows, :] = out
    if natural_out:
        _deinterleave(o_ref, y_scr)


def _conv_layer_kernel(*refs, natural_in, natural_out, tiles_per_seq, n_tiles):
    if natural_in:
        (hm_ref, g_ref, win_ref, dw_ref, dwb_ref, lng_ref, lnb_ref, wout_ref, o_ref,
         hp_scr, hn_scr, u_scr, carry_scr, z_scr, c_scr, v_scr, y_scr) = refs
    else:
        (hm_ref, he_ref, g_ref, win_ref, dw_ref, dwb_ref, lng_ref, lnb_ref, wout_ref, o_ref,
         hn_scr, u_scr, carry_scr, z_scr, c_scr, v_scr, y_scr) = refs
    T = hn_scr.shape[0]
    E = wout_ref.shape[0]
    g = pl.program_id(0)
    m = jnp.minimum(g, n_tiles - 1)
    slot = g % 2

    @pl.when(g == 0)
    def _():
        c_scr[...] = jnp.zeros(c_scr.shape, jnp.float32)
        z_scr[...] = jnp.zeros(z_scr.shape, jnp.float32)
        if natural_in:
            hp_scr[...] = jnp.zeros(hp_scr.shape, jnp.float32)

    @pl.when(m % tiles_per_seq == 0)
    def _():
        carry_scr[...] = jnp.zeros(carry_scr.shape, jnp.float32)

    if natural_in:
        h_main = hp_scr.at[slot]
        h_back = hp_scr.at[1 - slot]
        _interleave(h_main, hm_ref)
    else:
        h_main = hm_ref
        h_back = he_ref
    z_main = z_scr.at[slot]
    z_back = z_scr.at[1 - slot]
    _normalize_input(h_main, g_ref, hn_scr)

    half = T // 2
    for r0 in range(0, T, half):
        hn = hn_scr[r0:r0 + half, :]
        for c0 in range(0, E, COL_CHUNK):
            cols = slice(c0, c0 + COL_CHUNK)
            a = _dot(hn, win_ref[:, c0:c0 + COL_CHUNK])
            b = _dot(hn, win_ref[:, E + c0:E + c0 + COL_CHUNK])
            u_scr[CONV_HALO + r0:CONV_HALO + r0 + half, cols] = a * _sigmoid(b)
            z = _dot(hn, win_ref[:, 2 * E + c0:2 * E + c0 + COL_CHUNK])
            z_main[r0:r0 + half, cols] = z * _sigmoid(z)

    lng = lng_ref[...]
    lnb = lnb_ref[...]
    for h0 in range(0, T, OUT_ROWS):
        for r0 in range(h0, h0 + OUT_ROWS, ROW_CHUNK):
            rows = slice(r0, r0 + ROW_CHUNK)
            c = c_scr[rows, :]
            mu = jnp.mean(c, axis=-1, keepdims=True)
            cc = c - mu
            var = jnp.mean(cc * cc, axis=-1, keepdims=True)
            y = cc * lax.rsqrt(var + LN_EPS) * lng + lnb
            v_scr[rows, :] = (y * _sigmoid(y) * z_back[rows, :]).astype(jnp.bfloat16)
        y_scr[h0:h0 + OUT_ROWS, :] = _dot(v_scr[h0:h0 + OUT_ROWS, :], wout_ref[...])
    _residual_out(h_back, y_scr, o_ref, None, natural_out, False)

    _fill_history(u_scr, carry_scr, CONV_HALO, T, slice(0, E))
    for l0 in range(0, E, LANES):
        _conv_lane_tile(u_scr, dw_ref, dwb_ref, c_scr, slice(l0, l0 + LANES), T)


def _pool_layer_kernel(h_ref, g_ref, win_ref, wgrp_ref, bgrp_ref, scale_ref, wout_ref,
                       fg_ref, o_ref, hp_scr, hn_scr, u_scr, carry_scr, d_scr, v_scr,
                       y_scr, *, natural_in, natural_out, final_norm):
    T = hn_scr.shape[0]
    E = wout_ref.shape[0]
    gc = E // len(POOL_WINDOWS)

    @pl.when(pl.program_id(1) == 0)
    def _():
        carry_scr[...] = jnp.zeros(carry_scr.shape, jnp.float32)

    if natural_in:
        _interleave(hp_scr, h_ref)
        h_src = hp_scr
    else:
        h_src = h_ref
    _normalize_input(h_src, g_ref, hn_scr)

    t0 = pl.program_id(1) * T
    for g, w in enumerate(POOL_WINDOWS):
        cols = slice(g * gc, (g + 1) * gc)
        hn = hn_scr[...]
        u_scr[POOL_HALO:POOL_HALO + T, cols] = _dot(hn, win_ref[:, g * gc:(g + 1) * gc])
        z = _dot(hn, win_ref[:, E + g * gc:E + (g + 1) * gc])
        sz = z * _sigmoid(z)

        _fill_history(u_scr, carry_scr, POOL_HALO, T, cols)

        back = SUBLANES * (w - 1)
        for r0 in range(0, T, POOL_ROWS):
            p = u_scr[r0 + POOL_HALO - back:r0 + POOL_HALO + POOL_ROWS, cols]
            cur = p[back:, :]
            span = 1
            while span < w:
                shift = SUBLANES * span
                p = p[shift:, :] + p[:-shift, :]
                span *= 2
            short = min(max(SUBLANES * w - r0, 0), POOL_ROWS)
            if short < POOL_ROWS:
                d = p[short:, :] * (1.0 / w) - cur[short:, :]
                d_scr[r0 + short:r0 + POOL_ROWS, cols] = d.astype(jnp.bfloat16)
            if short:
                row = r0 + lax.broadcasted_iota(jnp.int32, (short, gc), 0)
                pos1 = t0 + (row % SUBLANES) * SEG + row // SUBLANES + 1
                ds = p[:short, :] / jnp.minimum(pos1, w).astype(jnp.float32) - cur[:short, :]
                d_scr[r0:r0 + short, cols] = ds.astype(jnp.bfloat16)

        yg = _dot(d_scr[:, cols], wgrp_ref[g])
        yg = (yg + bgrp_ref[:, cols]) * scale_ref[:, cols] * sz
        v_scr[:, cols] = yg.astype(jnp.bfloat16)

    half = T // 2
    for r0 in range(0, T, half):
        y_scr[r0:r0 + half, :] = _dot(v_scr[r0:r0 + half, :], wout_ref[...])
    _residual_out(h_src, y_scr, o_ref, fg_ref, natural_out, final_norm)


def _full_spec(shape):
    return pl.BlockSpec(shape, lambda b, s: (0,) * len(shape))


def _tile_spec(d):
    return pl.BlockSpec((None, TILE_T, d), lambda b, s: (b, s, 0))


def _compiler_params():
    return pltpu.CompilerParams(
        dimension_semantics=("arbitrary", "arbitrary"),
        vmem_limit_bytes=VMEM_LIMIT_BYTES,
    )


def _check_tiling(S, E):
    assert S % TILE_T == 0 and TILE_T % ROW_CHUNK == 0
    assert TILE_T % CONV_ROWS == 0 and TILE_T % POOL_ROWS == 0 and E % COL_CHUNK == 0
    assert SEG >= CONV_WIDTH - 1 and SEG >= max(POOL_WINDOWS) - 1


def _conv_layer(h, g, w_in, dw, dw_b, ln_g, ln_b, w_out, natural_in, natural_out):
    B, S, D = h.shape
    E = w_out.shape[0]
    _check_tiling(S, E)
    f32, bf16 = jnp.float32, jnp.bfloat16
    tiles_per_seq = S // TILE_T
    n_tiles = B * tiles_per_seq

    def front_tile(i):
        t = jnp.minimum(i, n_tiles - 1)
        return (t // tiles_per_seq, t % tiles_per_seq, 0)

    def back_tile(i):
        t = jnp.maximum(i - 1, 0)
        return (t // tiles_per_seq, t % tiles_per_seq, 0)

    def full(shape):
        return pl.BlockSpec(shape, lambda i: (0,) * len(shape))

    tile = (None, TILE_T, D)
    h_specs = [pl.BlockSpec(tile, front_tile)]
    h_args = [h]
    scratch = []
    if natural_in:
        scratch.append(pltpu.VMEM((2, TILE_T, D), f32))
    else:
        h_specs.append(pl.BlockSpec(tile, back_tile))
        h_args.append(h)
    scratch += [
        pltpu.VMEM((TILE_T, D), bf16),
        pltpu.VMEM((CONV_HALO + TILE_T, E), f32),
        pltpu.VMEM((CONV_HALO, E), f32),
        pltpu.VMEM((2, TILE_T, E), f32),
        pltpu.VMEM((TILE_T, E), f32),
        pltpu.VMEM((TILE_T, E), bf16),
        pltpu.VMEM((TILE_T, D), f32),
    ]
    return pl.pallas_call(
        functools.partial(_conv_layer_kernel, natural_in=natural_in, natural_out=natural_out,
                          tiles_per_seq=tiles_per_seq, n_tiles=n_tiles),
        grid=(n_tiles + 1,),
        in_specs=h_specs + [
            full((1, D)),
            full((D, 3 * E)),
            full((CONV_WIDTH, E)),
            full((1, E)),
            full((1, E)),
            full((1, E)),
            full((E, D)),
        ],
        out_specs=pl.BlockSpec(tile, back_tile),
        out_shape=jax.ShapeDtypeStruct((B, S, D), f32),
        scratch_shapes=scratch,
        compiler_params=pltpu.CompilerParams(
            dimension_semantics=("arbitrary",),
            vmem_limit_bytes=VMEM_LIMIT_BYTES,
        ),
        name="conv_layer_in" if natural_in else "conv_layer",
    )(*h_args, g.reshape(1, D), w_in.astype(bf16), dw, dw_b.reshape(1, E),
      ln_g.reshape(1, E), ln_b.reshape(1, E), w_out.astype(bf16))


def _pool_layer(h, g, w_in, w_grp, b_grp, scale, w_out, final_g, natural_in,
                natural_out, final_norm):
    B, S, D = h.shape
    E = w_out.shape[0]
    n_groups, gc, _ = w_grp.shape
    assert n_groups == len(POOL_WINDOWS)
    _check_tiling(S, E)
    f32, bf16 = jnp.float32, jnp.bfloat16
    return pl.pallas_call(
        functools.partial(_pool_layer_kernel, natural_in=natural_in,
                          natural_out=natural_out, final_norm=final_norm),
        grid=(B, S // TILE_T),
        in_specs=[
            _tile_spec(D),
            _full_spec((1, D)),
            _full_spec((D, 2 * E)),
            _full_spec((n_groups, gc, gc)),
            _full_spec((1, E)),
            _full_spec((1, E)),
            _full_spec((E, D)),
            _full_spec((1, D)),
        ],
        out_specs=_tile_spec(D),
        out_shape=jax.ShapeDtypeStruct((B, S, D), f32),
        scratch_shapes=[
            pltpu.VMEM((TILE_T if natural_in else SUBLANES, D), f32),
            pltpu.VMEM((TILE_T, D), bf16),
            pltpu.VMEM((POOL_HALO + TILE_T, E), f32),
            pltpu.VMEM((POOL_HALO, E), f32),
            pltpu.VMEM((TILE_T, E), bf16),
            pltpu.VMEM((TILE_T, E), bf16),
            pltpu.VMEM((TILE_T, D), f32),
        ],
        compiler_params=_compiler_params(),
        name="pool_layer_out" if natural_out else "pool_layer",
    )(h, g.reshape(1, D), w_in.astype(bf16), w_grp.astype(bf16),
      b_grp.reshape(1, E), scale.reshape(1, E), w_out.astype(bf16),
      final_g.reshape(1, D))


def kernel(x, norm_g, final_g, conv_w_in, conv_dw, conv_dw_b, conv_ln_g, conv_ln_b,
           conv_w_out, pool_w_in, pool_w_grp, pool_b_grp, pool_scale, pool_w_out):
    depth = norm_g.shape[0]
    h = x
    for i in range(depth):
        j = i // 2
        first, last = i == 0, i == depth - 1
        if i % 2 == 0:
            assert not last, "the final RMSNorm is fused into a pooling layer"
            h = _conv_layer(h, norm_g[i], conv_w_in[j], conv_dw[j], conv_dw_b[j],
                            conv_ln_g[j], conv_ln_b[j], conv_w_out[j],
                            natural_in=first, natural_out=last)
        else:
            h = _pool_layer(h, norm_g[i], pool_w_in[j], pool_w_grp[j], pool_b_grp[j],
                            pool_scale[j], pool_w_out[j], final_g,
                            natural_in=first, natural_out=last, final_norm=last)
    return h
```

```python
import functools

import jax
import jax.numpy as jnp
from jax import lax
from jax.experimental import pallas as pl
from jax.experimental.pallas import tpu as pltpu

RMS_EPS = 1e-6
LN_EPS = 1e-5
CONV_WIDTH = 31
POOL_WINDOWS = (2, 4, 8, 16)

SUBLANES = 8
LANES = 128

TILE_T = 512
SEG = TILE_T // SUBLANES
CONV_HALO = SUBLANES * (CONV_WIDTH - 1)
POOL_HALO = SUBLANES * (max(POOL_WINDOWS) - 1)
ROW_CHUNK = 64
CONV_ROWS = 128
CONV_TAP_GROUP = 4
PACKED_ROWS = 2 * SUBLANES
POOL_ROWS = 128
COL_CHUNK = 256
OUT_ROWS = 256
VMEM_LIMIT_BYTES = 56 * 1024 * 1024


def _sigmoid(x):
    return 1.0 / (1.0 + jnp.exp(-x))


def _rmsnorm_rows(x, g):
    ms = jnp.mean(x * x, axis=-1, keepdims=True)
    return x * lax.rsqrt(ms + RMS_EPS) * g


def _dot(a, b):
    return jnp.dot(a, b, preferred_element_type=jnp.float32)


def _interleave(dst_ref, src_ref):
    D = src_ref.shape[1]
    for jb in range(SEG // SUBLANES):
        x = jnp.stack([src_ref[i * SEG + jb * SUBLANES:i * SEG + (jb + 1) * SUBLANES, :]
                       for i in range(SUBLANES)], axis=0)
        y = jnp.transpose(x, (1, 0, 2)).reshape(SUBLANES * SUBLANES, D)
        dst_ref[jb * SUBLANES * SUBLANES:(jb + 1) * SUBLANES * SUBLANES, :] = y


def _deinterleave(dst_ref, src_ref):
    D = src_ref.shape[1]
    for jb in range(SEG // SUBLANES):
        y = src_ref[jb * SUBLANES * SUBLANES:(jb + 1) * SUBLANES * SUBLANES, :]
        x = jnp.transpose(y.reshape(SUBLANES, SUBLANES, D), (1, 0, 2))
        for i in range(SUBLANES):
            dst_ref[i * SEG + jb * SUBLANES:i * SEG + (jb + 1) * SUBLANES, :] = x[i]


def _normalize_input(h_ref, g_ref, hn_scr):
    g = g_ref[...]
    for r0 in range(0, hn_scr.shape[0], ROW_CHUNK):
        rows = slice(r0, r0 + ROW_CHUNK)
        hn_scr[rows, :] = _rmsnorm_rows(h_ref[rows, :], g).astype(jnp.bfloat16)


def _fill_history(u_scr, carry_scr, halo, T, cols):
    width = cols.stop - cols.start
    sub = lax.broadcasted_iota(jnp.int32, (SUBLANES, width), 0)
    for jj in range(halo // SUBLANES):
        rows = slice(jj * SUBLANES, (jj + 1) * SUBLANES)
        tail = u_scr[T + jj * SUBLANES:T + (jj + 1) * SUBLANES, cols]
        merged = jnp.where(sub == SUBLANES - 1, carry_scr[rows, cols], tail)
        u_scr[rows, cols] = pltpu.roll(merged, 1, axis=0)
        carry_scr[rows, cols] = tail


def _pack_conv_input(u_scr, ue_scr, uo_scr):
    n_rows = u_scr.shape[0]
    for r0 in range(0, n_rows, ROW_CHUNK):
        n = min(ROW_CHUNK, n_rows - r0)
        ue_scr[r0:r0 + n, :] = u_scr[r0:r0 + n, :].astype(jnp.bfloat16)
        n = min(ROW_CHUNK, n_rows - SUBLANES - r0)
        uo_scr[r0:r0 + n, :] = u_scr[r0 + SUBLANES:r0 + SUBLANES + n, :].astype(jnp.bfloat16)


def _conv_lane_tile(ue_scr, uo_scr, dw_ref, dwb_ref, c_scr, lanes, T):
    n_pairs = CONV_ROWS // PACKED_ROWS
    w = dw_ref[:, lanes]
    wb = [jnp.broadcast_to(w[k:k + 1, :], (PACKED_ROWS, LANES)).astype(jnp.bfloat16)
          for k in range(CONV_WIDTH)]
    bias = jnp.broadcast_to(dwb_ref[:, lanes], (PACKED_ROWS, LANES))

    def row_block(i, carry):
        r0 = pl.multiple_of(i * CONV_ROWS, CONV_ROWS)
        for p in range(n_pairs):
            out_row = r0 + p * PACKED_ROWS
            acc = bias
            for k0 in range(0, CONV_WIDTH, CONV_TAP_GROUP):
                part = None
                for k in range(k0, min(k0 + CONV_TAP_GROUP, CONV_WIDTH)):
                    if k % 2 == 0:
                        x = ue_scr[pl.ds(out_row + SUBLANES * k, PACKED_ROWS), lanes]
                    else:
                        x = uo_scr[pl.ds(out_row + SUBLANES * (k - 1), PACKED_ROWS), lanes]
                    t = x * wb[k]
                    part = t if part is None else part + t
                acc = acc + part.astype(jnp.float32)
            c_scr[pl.ds(out_row, PACKED_ROWS), lanes] = acc
        return carry
    lax.fori_loop(0, T // CONV_ROWS, row_block, 0)


def _residual_out(h_src, y_scr, o_ref, fg_ref, natural_out, final_norm):
    dst = y_scr if natural_out else o_ref
    for r0 in range(0, y_scr.shape[0], ROW_CHUNK):
        rows = slice(r0, r0 + ROW_CHUNK)
        out = h_src[rows, :] + y_scr[rows, :]
        if final_norm:
            out = _rmsnorm_rows(out, fg_ref[...])
        dst[rows, :] = out
    if natural_out:
        _deinterleave(o_ref, y_scr)


def _conv_layer_kernel(*refs, natural_in, natural_out, tiles_per_seq, n_tiles):
    if natural_in:
        (hm_ref, g_ref, win_ref, dw_ref, dwb_ref, lng_ref, lnb_ref, wout_ref, o_ref,
         hp_scr, hn_scr, u_scr, ue_scr, uo_scr, carry_scr, z_scr, c_scr, v_scr, y_scr) = refs
    else:
        (hm_ref, he_ref, g_ref, win_ref, dw_ref, dwb_ref, lng_ref, lnb_ref, wout_ref, o_ref,
         hn_scr, u_scr, ue_scr, uo_scr, carry_scr, z_scr, c_scr, v_scr, y_scr) = refs
    T = hn_scr.shape[0]
    E = wout_ref.shape[0]
    g = pl.program_id(0)
    m = jnp.minimum(g, n_tiles - 1)
    slot = g % 2

    @pl.when(g == 0)
    def _():
        c_scr[...] = jnp.zeros(c_scr.shape, jnp.float32)
        z_scr[...] = jnp.zeros(z_scr.shape, jnp.float32)
        if natural_in:
            hp_scr[...] = jnp.zeros(hp_scr.shape, jnp.float32)

    @pl.when(m % tiles_per_seq == 0)
    def _():
        carry_scr[...] = jnp.zeros(carry_scr.shape, jnp.float32)

    if natural_in:
        h_main = hp_scr.at[slot]
        h_back = hp_scr.at[1 - slot]
        _interleave(h_main, hm_ref)
    else:
        h_main = hm_ref
        h_back = he_ref
    z_main = z_scr.at[slot]
    z_back = z_scr.at[1 - slot]
    _normalize_input(h_main, g_ref, hn_scr)

    half = T // 2
    for r0 in range(0, T, half):
        hn = hn_scr[r0:r0 + half, :]
        for c0 in range(0, E, COL_CHUNK):
            cols = slice(c0, c0 + COL_CHUNK)
            a = _dot(hn, win_ref[:, c0:c0 + COL_CHUNK])
            b = _dot(hn, win_ref[:, E + c0:E + c0 + COL_CHUNK])
            u_scr[CONV_HALO + r0:CONV_HALO + r0 + half, cols] = a * _sigmoid(b)
            z = _dot(hn, win_ref[:, 2 * E + c0:2 * E + c0 + COL_CHUNK])
            z_main[r0:r0 + half, cols] = z * _sigmoid(z)

    lng = lng_ref[...]
    lnb = lnb_ref[...]
    for h0 in range(0, T, OUT_ROWS):
        for r0 in range(h0, h0 + OUT_ROWS, ROW_CHUNK):
            rows = slice(r0, r0 + ROW_CHUNK)
            c = c_scr[rows, :]
            mu = jnp.mean(c, axis=-1, keepdims=True)
            cc = c - mu
            var = jnp.mean(cc * cc, axis=-1, keepdims=True)
            y = cc * lax.rsqrt(var + LN_EPS) * lng + lnb
            v_scr[rows, :] = (y * _sigmoid(y) * z_back[rows, :]).astype(jnp.bfloat16)
        y_scr[h0:h0 + OUT_ROWS, :] = _dot(v_scr[h0:h0 + OUT_ROWS, :], wout_ref[...])
    _residual_out(h_back, y_scr, o_ref, None, natural_out, False)

    _fill_history(u_scr, carry_scr, CONV_HALO, T, slice(0, E))
    _pack_conv_input(u_scr, ue_scr, uo_scr)
    for l0 in range(0, E, LANES):
        _conv_lane_tile(ue_scr, uo_scr, dw_ref, dwb_ref, c_scr, slice(l0, l0 + LANES), T)


def _pool_layer_kernel(h_ref, g_ref, win_ref, wgrp_ref, bgrp_ref, scale_ref, wout_ref,
                       fg_ref, o_ref, hp_scr, hn_scr, u_scr, carry_scr, d_scr, v_scr,
                       y_scr, *, natural_in, natural_out, final_norm):
    T = hn_scr.shape[0]
    E = wout_ref.shape[0]
    gc = E // len(POOL_WINDOWS)

    @pl.when(pl.program_id(1) == 0)
    def _():
        carry_scr[...] = jnp.zeros(carry_scr.shape, jnp.float32)

    if natural_in:
        _interleave(hp_scr, h_ref)
        h_src = hp_scr
    else:
        h_src = h_ref
    _normalize_input(h_src, g_ref, hn_scr)

    t0 = pl.program_id(1) * T
    for g, w in enumerate(POOL_WINDOWS):
        cols = slice(g * gc, (g + 1) * gc)
        hn = hn_scr[...]
        u_scr[POOL_HALO:POOL_HALO + T, cols] = _dot(hn, win_ref[:, g * gc:(g + 1) * gc])
        z = _dot(hn, win_ref[:, E + g * gc:E + (g + 1) * gc])
        sz = z * _sigmoid(z)

        _fill_history(u_scr, carry_scr, POOL_HALO, T, cols)

        back = SUBLANES * (w - 1)
        for r0 in range(0, T, POOL_ROWS):
            p = u_scr[r0 + POOL_HALO - back:r0 + POOL_HALO + POOL_ROWS, cols]
            cur = p[back:, :]
            span = 1
            while span < w:
                shift = SUBLANES * span
                p = p[shift:, :] + p[:-shift, :]
                span *= 2
            short = min(max(SUBLANES * w - r0, 0), POOL_ROWS)
            if short < POOL_ROWS:
                d = p[short:, :] * (1.0 / w) - cur[short:, :]
                d_scr[r0 + short:r0 + POOL_ROWS, cols] = d.astype(jnp.bfloat16)
            if short:
                row = r0 + lax.broadcasted_iota(jnp.int32, (short, gc), 0)
                pos1 = t0 + (row % SUBLANES) * SEG + row // SUBLANES + 1
                ds = p[:short, :] / jnp.minimum(pos1, w).astype(jnp.float32) - cur[:short, :]
                d_scr[r0:r0 + short, cols] = ds.astype(jnp.bfloat16)

        yg = _dot(d_scr[:, cols], wgrp_ref[g])
        yg = (yg + bgrp_ref[:, cols]) * scale_ref[:, cols] * sz
        v_scr[:, cols] = yg.astype(jnp.bfloat16)

    half = T // 2
    for r0 in range(0, T, half):
        y_scr[r0:r0 + half, :] = _dot(v_scr[r0:r0 + half, :], wout_ref[...])
    _residual_out(h_src, y_scr, o_ref, fg_ref, natural_out, final_norm)


def _full_spec(shape):
    return pl.BlockSpec(shape, lambda b, s: (0,) * len(shape))


def _tile_spec(d):
    return pl.BlockSpec((None, TILE_T, d), lambda b, s: (b, s, 0))


def _compiler_params():
    return pltpu.CompilerParams(
        dimension_semantics=("arbitrary", "arbitrary"),
        vmem_limit_bytes=VMEM_LIMIT_BYTES,
    )


def _check_tiling(S, E):
    assert S % TILE_T == 0 and TILE_T % ROW_CHUNK == 0
    assert TILE_T % CONV_ROWS == 0 and TILE_T % POOL_ROWS == 0 and E % COL_CHUNK == 0
    assert SEG >= CONV_WIDTH - 1 and SEG >= max(POOL_WINDOWS) - 1
    assert TILE_T % PACKED_ROWS == 0 and CONV_HALO % PACKED_ROWS == 0 and CONV_ROWS % PACKED_ROWS == 0


def _conv_layer(h, g, w_in, dw, dw_b, ln_g, ln_b, w_out, natural_in, natural_out):
    B, S, D = h.shape
    E = w_out.shape[0]
    _check_tiling(S, E)
    f32, bf16 = jnp.float32, jnp.bfloat16
    tiles_per_seq = S // TILE_T
    n_tiles = B * tiles_per_seq

    def front_tile(i):
        t = jnp.minimum(i, n_tiles - 1)
        return (t // tiles_per_seq, t % tiles_per_seq, 0)

    def back_tile(i):
        t = jnp.maximum(i - 1, 0)
        return (t // tiles_per_seq, t % tiles_per_seq, 0)

    def full(shape):
        return pl.BlockSpec(shape, lambda i: (0,) * len(shape))

    tile = (None, TILE_T, D)
    h_specs = [pl.BlockSpec(tile, front_tile)]
    h_args = [h]
    scratch = []
    if natural_in:
        scratch.append(pltpu.VMEM((2, TILE_T, D), f32))
    else:
        h_specs.append(pl.BlockSpec(tile, back_tile))
        h_args.append(h)
    scratch += [
        pltpu.VMEM((TILE_T, D), bf16),
        pltpu.VMEM((CONV_HALO + TILE_T, E), f32),
        pltpu.VMEM((CONV_HALO + TILE_T, E), bf16),
        pltpu.VMEM((CONV_HALO + TILE_T, E), bf16),
        pltpu.VMEM((CONV_HALO, E), f32),
        pltpu.VMEM((2, TILE_T, E), f32),
        pltpu.VMEM((TILE_T, E), f32),
        pltpu.VMEM((TILE_T, E), bf16),
        pltpu.VMEM((TILE_T, D), f32),
    ]
    return pl.pallas_call(
        functools.partial(_conv_layer_kernel, natural_in=natural_in, natural_out=natural_out,
                          tiles_per_seq=tiles_per_seq, n_tiles=n_tiles),
        grid=(n_tiles + 1,),
        in_specs=h_specs + [
            full((1, D)),
            full((D, 3 * E)),
            full((CONV_WIDTH, E)),
            full((1, E)),
            full((1, E)),
            full((1, E)),
            full((E, D)),
        ],
        out_specs=pl.BlockSpec(tile, back_tile),
        out_shape=jax.ShapeDtypeStruct((B, S, D), f32),
        scratch_shapes=scratch,
        compiler_params=pltpu.CompilerParams(
            dimension_semantics=("arbitrary",),
            vmem_limit_bytes=VMEM_LIMIT_BYTES,
        ),
        name="conv_layer_in" if natural_in else "conv_layer",
    )(*h_args, g.reshape(1, D), w_in.astype(bf16), dw, dw_b.reshape(1, E),
      ln_g.reshape(1, E), ln_b.reshape(1, E), w_out.astype(bf16))


def _pool_layer(h, g, w_in, w_grp, b_grp, scale, w_out, final_g, natural_in,
                natural_out, final_norm):
    B, S, D = h.shape
    E = w_out.shape[0]
    n_groups, gc, _ = w_grp.shape
    assert n_groups == len(POOL_WINDOWS)
    _check_tiling(S, E)
    f32, bf16 = jnp.float32, jnp.bfloat16
    return pl.pallas_call(
        functools.partial(_pool_layer_kernel, natural_in=natural_in,
                          natural_out=natural_out, final_norm=final_norm),
        grid=(B, S // TILE_T),
        in_specs=[
            _tile_spec(D),
            _full_spec((1, D)),
            _full_spec((D, 2 * E)),
            _full_spec((n_groups, gc, gc)),
            _full_spec((1, E)),
            _full_spec((1, E)),
            _full_spec((E, D)),
            _full_spec((1, D)),
        ],
        out_specs=_tile_spec(D),
        out_shape=jax.ShapeDtypeStruct((B, S, D), f32),
        scratch_shapes=[
            pltpu.VMEM((TILE_T if natural_in else SUBLANES, D), f32),
            pltpu.VMEM((TILE_T, D), bf16),
            pltpu.VMEM((POOL_HALO + TILE_T, E), f32),
            pltpu.VMEM((POOL_HALO, E), f32),
            pltpu.VMEM((TILE_T, E), bf16),
            pltpu.VMEM((TILE_T, E), bf16),
            pltpu.VMEM((TILE_T, D), f32),
        ],
        compiler_params=_compiler_params(),
        name="pool_layer_out" if natural_out else "pool_layer",
    )(h, g.reshape(1, D), w_in.astype(bf16), w_grp.astype(bf16),
      b_grp.reshape(1, E), scale.reshape(1, E), w_out.astype(bf16),
      final_g.reshape(1, D))


def kernel(x, norm_g, final_g, conv_w_in, conv_dw, conv_dw_b, conv_ln_g, conv_ln_b,
           conv_w_out, pool_w_in, pool_w_grp, pool_b_grp, pool_scale, pool_w_out):
    depth = norm_g.shape[0]
    h = x
    for i in range(depth):
        j = i // 2
        first, last = i == 0, i == depth - 1
        if i % 2 == 0:
            assert not last, "the final RMSNorm is fused into a pooling layer"
            h = _conv_layer(h, norm_g[i], conv_w_in[j], conv_dw[j], conv_dw_b[j],
                            conv_ln_g[j], conv_ln_b[j], conv_w_out[j],
                            natural_in=first, natural_out=last)
        else:
            h = _pool_layer(h, norm_g[i], pool_w_in[j], pool_w_grp[j], pool_b_grp[j],
                            pool_scale[j], pool_w_out[j], final_g,
                            natural_in=first, natural_out=last, final_norm=last)
    return h
```

```python
import functools

import jax
import jax.numpy as jnp
from jax import lax
from jax.experimental import pallas as pl
from jax.experimental.pallas import tpu as pltpu

RMS_EPS = 1e-6
LN_EPS = 1e-5
CONV_WIDTH = 31
POOL_WINDOWS = (2, 4, 8, 16)

SUBLANES = 8
LANES = 128

TILE_T = 1024
SEG = TILE_T // SUBLANES
CONV_HALO = SUBLANES * (CONV_WIDTH - 1)
POOL_HALO = SUBLANES * (max(POOL_WINDOWS) - 1)
ROW_CHUNK = 64
CONV_ROWS = 64
CONV_ACC_SPLIT = 4
POOL_ROWS = 128
COL_CHUNK = 256
OUT_ROWS = 256
VMEM_LIMIT_BYTES = 60 * 1024 * 1024


def _sigmoid(x):
    return 1.0 / (1.0 + jnp.exp(-x))


def _rmsnorm_rows(x, g):
    ms = jnp.mean(x * x, axis=-1, keepdims=True)
    return x * lax.rsqrt(ms + RMS_EPS) * g


def _dot(a, b):
    return jnp.dot(a, b, preferred_element_type=jnp.float32)


def _interleave(dst_ref, src_ref):
    D = src_ref.shape[1]
    for jb in range(SEG // SUBLANES):
        x = jnp.stack([src_ref[i * SEG + jb * SUBLANES:i * SEG + (jb + 1) * SUBLANES, :]
                       for i in range(SUBLANES)], axis=0)
        y = jnp.transpose(x, (1, 0, 2)).reshape(SUBLANES * SUBLANES, D)
        dst_ref[jb * SUBLANES * SUBLANES:(jb + 1) * SUBLANES * SUBLANES, :] = y


def _deinterleave(dst_ref, src_ref):
    D = src_ref.shape[1]
    for jb in range(SEG // SUBLANES):
        y = src_ref[jb * SUBLANES * SUBLANES:(jb + 1) * SUBLANES * SUBLANES, :]
        x = jnp.transpose(y.reshape(SUBLANES, SUBLANES, D), (1, 0, 2))
        for i in range(SUBLANES):
            dst_ref[i * SEG + jb * SUBLANES:i * SEG + (jb + 1) * SUBLANES, :] = x[i]


def _normalize_input(h_ref, g_ref, hn_scr):
    g = g_ref[...]
    for r0 in range(0, hn_scr.shape[0], ROW_CHUNK):
        rows = slice(r0, r0 + ROW_CHUNK)
        hn_scr[rows, :] = _rmsnorm_rows(h_ref[rows, :], g).astype(jnp.bfloat16)


def _fill_history(u_scr, carry_scr, halo, T, cols):
    width = cols.stop - cols.start
    sub = lax.broadcasted_iota(jnp.int32, (SUBLANES, width), 0)
    for jj in range(halo // SUBLANES):
        rows = slice(jj * SUBLANES, (jj + 1) * SUBLANES)
        tail = u_scr[T + jj * SUBLANES:T + (jj + 1) * SUBLANES, cols]
        merged = jnp.where(sub == SUBLANES - 1, carry_scr[rows, cols], tail)
        u_scr[rows, cols] = pltpu.roll(merged, 1, axis=0)
        carry_scr[rows, cols] = tail


def _conv_lane_tile(u_scr, dw_ref, dwb_ref, c_scr, lanes, T):
    n_out = CONV_ROWS // SUBLANES
    w = dw_ref[:, lanes]
    wb = [jnp.broadcast_to(w[k:k + 1, :], (SUBLANES, LANES)) for k in range(CONV_WIDTH)]
    bias = jnp.broadcast_to(dwb_ref[:, lanes], (SUBLANES, LANES))

    def row_block(i, carry):
        r0 = pl.multiple_of(i * CONV_ROWS, CONV_ROWS)
        acc = [[bias] + [None] * (CONV_ACC_SPLIT - 1) for _ in range(n_out)]
        for m in range(n_out + CONV_WIDTH - 1):
            x = u_scr[pl.ds(r0 + m * SUBLANES, SUBLANES), lanes]
            for o in range(n_out):
                k = m - o
                if 0 <= k < CONV_WIDTH:
                    p = x * wb[k]
                    part = acc[o][k % CONV_ACC_SPLIT]
                    acc[o][k % CONV_ACC_SPLIT] = p if part is None else part + p
        outs = [functools.reduce(lambda s, t: s + t, parts) for parts in acc]
        c_scr[pl.ds(r0, CONV_ROWS), lanes] = jnp.concatenate(outs, axis=0)
        return carry
    lax.fori_loop(0, T // CONV_ROWS, row_block, 0)


def _residual_out(h_src, y_scr, o_ref, fg_ref, natural_out, final_norm):
    dst = y_scr if natural_out else o_ref
    for r0 in range(0, y_scr.shape[0], ROW_CHUNK):
        rows = slice(r0, r0 + ROW_CHUNK)
        out = h_src[rows, :] + y_scr[rows, :]
        if final_norm:
            out = _rmsnorm_rows(out, fg_ref[...])
        dst[rows, :] = out
    if natural_out:
        _deinterleave(o_ref, y_scr)


def _conv_layer_kernel(*refs, natural_in, natural_out, tiles_per_seq, n_tiles):
    if natural_in:
        (hm_ref, g_ref, win_ref, dw_ref, dwb_ref, lng_ref, lnb_ref, wout_ref, o_ref,
         hp_scr, hn_scr, u_scr, carry_scr, z_scr, c_scr, v_scr, y_scr) = refs
    else:
        (hm_ref, he_ref, g_ref, win_ref, dw_ref, dwb_ref, lng_ref, lnb_ref, wout_ref, o_ref,
         hn_scr, u_scr, carry_scr, z_scr, c_scr, v_scr, y_scr) = refs
    T = hn_scr.shape[0]
    E = wout_ref.shape[0]
    g = pl.program_id(0)
    m = jnp.minimum(g, n_tiles - 1)
    slot = g % 2

    @pl.when(g == 0)
    def _():
        c_scr[...] = jnp.zeros(c_scr.shape, jnp.float32)
        z_scr[...] = jnp.zeros(z_scr.shape, jnp.float32)
        if natural_in:
            hp_scr[...] = jnp.zeros(hp_scr.shape, jnp.float32)

    @pl.when(m % tiles_per_seq == 0)
    def _():
        carry_scr[...] = jnp.zeros(carry_scr.shape, jnp.float32)

    if natural_in:
        h_main = hp_scr.at[slot]
        h_back = hp_scr.at[1 - slot]
        _interleave(h_main, hm_ref)
    else:
        h_main = hm_ref
        h_back = he_ref
    z_main = z_scr.at[slot]
    z_back = z_scr.at[1 - slot]
    _normalize_input(h_main, g_ref, hn_scr)

    half = T // 2
    for r0 in range(0, T, half):
        hn = hn_scr[r0:r0 + half, :]
        for c0 in range(0, E, COL_CHUNK):
            cols = slice(c0, c0 + COL_CHUNK)
            a = _dot(hn, win_ref[:, c0:c0 + COL_CHUNK])
            b = _dot(hn, win_ref[:, E + c0:E + c0 + COL_CHUNK])
            u_scr[CONV_HALO + r0:CONV_HALO + r0 + half, cols] = a * _sigmoid(b)
            z = _dot(hn, win_ref[:, 2 * E + c0:2 * E + c0 + COL_CHUNK])
            z_main[r0:r0 + half, cols] = z * _sigmoid(z)

    lng = lng_ref[...]
    lnb = lnb_ref[...]
    for h0 in range(0, T, OUT_ROWS):
        for r0 in range(h0, h0 + OUT_ROWS, ROW_CHUNK):
            rows = slice(r0, r0 + ROW_CHUNK)
            c = c_scr[rows, :]
            mu = jnp.mean(c, axis=-1, keepdims=True)
            cc = c - mu
            var = jnp.mean(cc * cc, axis=-1, keepdims=True)
            y = cc * lax.rsqrt(var + LN_EPS) * lng + lnb
            v_scr[rows, :] = (y * _sigmoid(y) * z_back[rows, :]).astype(jnp.bfloat16)
        y_scr[h0:h0 + OUT_ROWS, :] = _dot(v_scr[h0:h0 + OUT_ROWS, :], wout_ref[...])
    _residual_out(h_back, y_scr, o_ref, None, natural_out, False)

    _fill_history(u_scr, carry_scr, CONV_HALO, T, slice(0, E))
    for l0 in range(0, E, LANES):
        _conv_lane_tile(u_scr, dw_ref, dwb_ref, c_scr, slice(l0, l0 + LANES), T)


def _pool_layer_kernel(h_ref, g_ref, win_ref, wgrp_ref, bgrp_ref, scale_ref, wout_ref,
                       fg_ref, o_ref, hp_scr, hn_scr, u_scr, carry_scr, d_scr, v_scr,
                       y_scr, *, natural_in, natural_out, final_norm):
    T = hn_scr.shape[0]
    E = wout_ref.shape[0]
    gc = E // len(POOL_WINDOWS)

    @pl.when(pl.program_id(1) == 0)
    def _():
        carry_scr[...] = jnp.zeros(carry_scr.shape, jnp.float32)

    if natural_in:
        _interleave(hp_scr, h_ref)
        h_src = hp_scr
    else:
        h_src = h_ref
    _normalize_input(h_src, g_ref, hn_scr)

    t0 = pl.program_id(1) * T
    for g, w in enumerate(POOL_WINDOWS):
        cols = slice(g * gc, (g + 1) * gc)
        hn = hn_scr[...]
        u_scr[POOL_HALO:POOL_HALO + T, cols] = _dot(hn, win_ref[:, g * gc:(g + 1) * gc])
        z = _dot(hn, win_ref[:, E + g * gc:E + (g + 1) * gc])
        sz = z * _sigmoid(z)

        _fill_history(u_scr, carry_scr, POOL_HALO, T, cols)

        back = SUBLANES * (w - 1)
        for r0 in range(0, T, POOL_ROWS):
            p = u_scr[r0 + POOL_HALO - back:r0 + POOL_HALO + POOL_ROWS, cols]
            cur = p[back:, :]
            span = 1
            while span < w:
                shift = SUBLANES * span
                p = p[shift:, :] + p[:-shift, :]
                span *= 2
            short = min(max(SUBLANES * w - r0, 0), POOL_ROWS)
            if short < POOL_ROWS:
                d = p[short:, :] * (1.0 / w) - cur[short:, :]
                d_scr[r0 + short:r0 + POOL_ROWS, cols] = d.astype(jnp.bfloat16)
            if short:
                row = r0 + lax.broadcasted_iota(jnp.int32, (short, gc), 0)
                pos1 = t0 + (row % SUBLANES) * SEG + row // SUBLANES + 1
                ds = p[:short, :] / jnp.minimum(pos1, w).astype(jnp.float32) - cur[:short, :]
                d_scr[r0:r0 + short, cols] = ds.astype(jnp.bfloat16)

        yg = _dot(d_scr[:, cols], wgrp_ref[g])
        yg = (yg + bgrp_ref[:, cols]) * scale_ref[:, cols] * sz
        v_scr[:, cols] = yg.astype(jnp.bfloat16)

    half = T // 2
    for r0 in range(0, T, half):
        y_scr[r0:r0 + half, :] = _dot(v_scr[r0:r0 + half, :], wout_ref[...])
    _residual_out(h_src, y_scr, o_ref, fg_ref, natural_out, final_norm)


def _full_spec(shape):
    return pl.BlockSpec(shape, lambda b, s: (0,) * len(shape))


def _tile_spec(d):
    return pl.BlockSpec((None, TILE_T, d), lambda b, s: (b, s, 0))


def _compiler_params():
    return pltpu.CompilerParams(
        dimension_semantics=("arbitrary", "arbitrary"),
        vmem_limit_bytes=VMEM_LIMIT_BYTES,
    )


def _check_tiling(S, E):
    assert S % TILE_T == 0 and TILE_T % ROW_CHUNK == 0
    assert TILE_T % CONV_ROWS == 0 and TILE_T % POOL_ROWS == 0 and E % COL_CHUNK == 0
    assert SEG >= CONV_WIDTH - 1 and SEG >= max(POOL_WINDOWS) - 1


def _conv_layer(h, g, w_in, dw, dw_b, ln_g, ln_b, w_out, natural_in, natural_out):
    B, S, D = h.shape
    E = w_out.shape[0]
    _check_tiling(S, E)
    f32, bf16 = jnp.float32, jnp.bfloat16
    tiles_per_seq = S // TILE_T
    n_tiles = B * tiles_per_seq

    def front_tile(i):
        t = jnp.minimum(i, n_tiles - 1)
        return (t // tiles_per_seq, t % tiles_per_seq, 0)

    def back_tile(i):
        t = jnp.maximum(i - 1, 0)
        return (t // tiles_per_seq, t % tiles_per_seq, 0)

    def full(shape):
        return pl.BlockSpec(shape, lambda i: (0,) * len(shape))

    tile = (None, TILE_T, D)
    h_specs = [pl.BlockSpec(tile, front_tile)]
    h_args = [h]
    scratch = []
    if natural_in:
        scratch.append(pltpu.VMEM((2, TILE_T, D), f32))
    else:
        h_specs.append(pl.BlockSpec(tile, back_tile))
        h_args.append(h)
    scratch += [
        pltpu.VMEM((TILE_T, D), bf16),
        pltpu.VMEM((CONV_HALO + TILE_T, E), f32),
        pltpu.VMEM((CONV_HALO, E), f32),
        pltpu.VMEM((2, TILE_T, E), f32),
        pltpu.VMEM((TILE_T, E), f32),
        pltpu.VMEM((TILE_T, E), bf16),
        pltpu.VMEM((TILE_T, D), f32),
    ]
    return pl.pallas_call(
        functools.partial(_conv_layer_kernel, natural_in=natural_in, natural_out=natural_out,
                          tiles_per_seq=tiles_per_seq, n_tiles=n_tiles),
        grid=(n_tiles + 1,),
        in_specs=h_specs + [
            full((1, D)),
            full((D, 3 * E)),
            full((CONV_WIDTH, E)),
            full((1, E)),
            full((1, E)),
            full((1, E)),
            full((E, D)),
        ],
        out_specs=pl.BlockSpec(tile, back_tile),
        out_shape=jax.ShapeDtypeStruct((B, S, D), f32),
        scratch_shapes=scratch,
        compiler_params=pltpu.CompilerParams(
            dimension_semantics=("arbitrary",),
            vmem_limit_bytes=VMEM_LIMIT_BYTES,
        ),
        name="conv_layer_in" if natural_in else "conv_layer",
    )(*h_args, g.reshape(1, D), w_in.astype(bf16), dw, dw_b.reshape(1, E),
      ln_g.reshape(1, E), ln_b.reshape(1, E), w_out.astype(bf16))


def _pool_layer(h, g, w_in, w_grp, b_grp, scale, w_out, final_g, natural_in,
                natural_out, final_norm):
    B, S, D = h.shape
    E = w_out.shape[0]
    n_groups, gc, _ = w_grp.shape
    assert n_groups == len(POOL_WINDOWS)
    _check_tiling(S, E)
    f32, bf16 = jnp.float32, jnp.bfloat16
    return pl.pallas_call(
        functools.partial(_pool_layer_kernel, natural_in=natural_in,
                          natural_out=natural_out, final_norm=final_norm),
        grid=(B, S // TILE_T),
        in_specs=[
            _tile_spec(D),
            _full_spec((1, D)),
            _full_spec((D, 2 * E)),
            _full_spec((n_groups, gc, gc)),
            _full_spec((1, E)),
            _full_spec((1, E)),
            _full_spec((E, D)),
            _full_spec((1, D)),
        ],
        out_specs=_tile_spec(D),
        out_shape=jax.ShapeDtypeStruct((B, S, D), f32),
        scratch_shapes=[
            pltpu.VMEM((TILE_T if natural_in else SUBLANES, D), f32),
            pltpu.VMEM((TILE_T, D), bf16),
            pltpu.VMEM((POOL_HALO + TILE_T, E), f32),
            pltpu.VMEM((POOL_HALO, E), f32),
            pltpu.VMEM((TILE_T, E), bf16),
            pltpu.VMEM((TILE_T, E), bf16),
            pltpu.VMEM((TILE_T, D), f32),
        ],
        compiler_params=_compiler_params(),
        name="pool_layer_out" if natural_out else "pool_layer",
    )(h, g.reshape(1, D), w_in.astype(bf16), w_grp.astype(bf16),
      b_grp.reshape(1, E), scale.reshape(1, E), w_out.astype(bf16),
      final_g.reshape(1, D))


def kernel(x, norm_g, final_g, conv_w_in, conv_dw, conv_dw_b, conv_ln_g, conv_ln_b,
           conv_w_out, pool_w_in, pool_w_grp, pool_b_grp, pool_scale, pool_w_out):
    depth = norm_g.shape[0]
    h = x
    for i in range(depth):
        j = i // 2
        first, last = i == 0, i == depth - 1
        if i % 2 == 0:
            assert not last, "the final RMSNorm is fused into a pooling layer"
            h = _conv_layer(h, norm_g[i], conv_w_in[j], conv_dw[j], conv_dw_b[j],
                            conv_ln_g[j], conv_ln_b[j], conv_w_out[j],
                            natural_in=first, natural_out=last)
        else:
            h = _pool_layer(h, norm_g[i], pool_w_in[j], pool_w_grp[j], pool_b_grp[j],
                            pool_scale[j], pool_w_out[j], final_g,
                            natural_in=first, natural_out=last, final_norm=last)
    return h
```

```python
import functools

import jax
import jax.numpy as jnp
from jax import lax
from jax.experimental import pallas as pl
from jax.experimental.pallas import tpu as pltpu

RMS_EPS = 1e-6
LN_EPS = 1e-5
CONV_WIDTH = 31
POOL_WINDOWS = (2, 4, 8, 16)

SUBLANES = 8
LANES = 128

TILE_T = 1024
SEG = TILE_T // SUBLANES
CONV_HALO = SUBLANES * (CONV_WIDTH - 1)
POOL_HALO = SUBLANES * (max(POOL_WINDOWS) - 1)
ROW_CHUNK = 64
CONV_ROWS = 64
CONV_ACC_SPLIT = 4
POOL_ROWS = 128
COL_CHUNK = 256
OUT_ROWS = 256
VMEM_LIMIT_BYTES = 60 * 1024 * 1024


def _sigmoid(x):
    return 1.0 / (1.0 + jnp.exp(-x))


def _rmsnorm_rows(x, g):
    ms = jnp.mean(x * x, axis=-1, keepdims=True)
    return x * lax.rsqrt(ms + RMS_EPS) * g


def _dot(a, b):
    return jnp.dot(a, b, preferred_element_type=jnp.float32)


def _interleave(dst_ref, src_ref):
    D = src_ref.shape[1]
    for jb in range(SEG // SUBLANES):
        x = jnp.stack([src_ref[i * SEG + jb * SUBLANES:i * SEG + (jb + 1) * SUBLANES, :]
                       for i in range(SUBLANES)], axis=0)
        y = jnp.transpose(x, (1, 0, 2)).reshape(SUBLANES * SUBLANES, D)
        dst_ref[jb * SUBLANES * SUBLANES:(jb + 1) * SUBLANES * SUBLANES, :] = y


def _deinterleave(dst_ref, src_ref):
    D = src_ref.shape[1]
    for jb in range(SEG // SUBLANES):
        y = src_ref[jb * SUBLANES * SUBLANES:(jb + 1) * SUBLANES * SUBLANES, :]
        x = jnp.transpose(y.reshape(SUBLANES, SUBLANES, D), (1, 0, 2))
        for i in range(SUBLANES):
            dst_ref[i * SEG + jb * SUBLANES:i * SEG + (jb + 1) * SUBLANES, :] = x[i]


def _normalize_input(h_ref, g_ref, hn_scr):
    g = g_ref[...]
    for r0 in range(0, hn_scr.shape[0], ROW_CHUNK):
        rows = slice(r0, r0 + ROW_CHUNK)
        hn_scr[rows, :] = _rmsnorm_rows(h_ref[rows, :], g).astype(jnp.bfloat16)


def _fill_history(u_scr, carry_scr, halo, T, cols):
    width = cols.stop - cols.start
    sub = lax.broadcasted_iota(jnp.int32, (SUBLANES, width), 0)
    for jj in range(halo // SUBLANES):
        rows = slice(jj * SUBLANES, (jj + 1) * SUBLANES)
        tail = u_scr[T + jj * SUBLANES:T + (jj + 1) * SUBLANES, cols]
        merged = jnp.where(sub == SUBLANES - 1, carry_scr[rows, cols], tail)
        u_scr[rows, cols] = pltpu.roll(merged, 1, axis=0)
        carry_scr[rows, cols] = tail


def _depthwise_conv(u_scr, dw_ref, dwb_ref, c_scr, T):
    n_out = CONV_ROWS // SUBLANES
    E = u_scr.shape[1]

    def lane_tile(l, carry):
        lanes = pl.ds(pl.multiple_of(l * LANES, LANES), LANES)
        w = dw_ref[:, lanes]
        wb = [jnp.broadcast_to(w[k:k + 1, :], (SUBLANES, LANES)) for k in range(CONV_WIDTH)]
        bias = jnp.broadcast_to(dwb_ref[:, lanes], (SUBLANES, LANES))
        for r0 in range(0, T, CONV_ROWS):
            acc = [[bias] + [None] * (CONV_ACC_SPLIT - 1) for _ in range(n_out)]
            for m in range(n_out + CONV_WIDTH - 1):
                x = u_scr[r0 + m * SUBLANES:r0 + (m + 1) * SUBLANES, lanes]
                for o in range(n_out):
                    k = m - o
                    if 0 <= k < CONV_WIDTH:
                        p = x * wb[k]
                        part = acc[o][k % CONV_ACC_SPLIT]
                        acc[o][k % CONV_ACC_SPLIT] = p if part is None else part + p
            outs = [functools.reduce(lambda s, t: s + t, parts) for parts in acc]
            c_scr[r0:r0 + CONV_ROWS, lanes] = jnp.concatenate(outs, axis=0)
        return carry
    lax.fori_loop(0, E // LANES, lane_tile, 0)


def _residual_out(h_src, y_scr, o_ref, fg_ref, natural_out, final_norm):
    dst = y_scr if natural_out else o_ref
    for r0 in range(0, y_scr.shape[0], ROW_CHUNK):
        rows = slice(r0, r0 + ROW_CHUNK)
        out = h_src[rows, :] + y_scr[rows, :]
        if final_norm:
            out = _rmsnorm_rows(out, fg_ref[...])
        dst[rows, :] = out
    if natural_out:
        _deinterleave(o_ref, y_scr)


def _conv_layer_kernel(*refs, natural_in, natural_out, tiles_per_seq, n_tiles):
    if natural_in:
        (hm_ref, g_ref, win_ref, dw_ref, dwb_ref, lng_ref, lnb_ref, wout_ref, o_ref,
         hp_scr, hn_scr, u_scr, carry_scr, z_scr, c_scr, v_scr, y_scr) = refs
    else:
        (hm_ref, he_ref, g_ref, win_ref, dw_ref, dwb_ref, lng_ref, lnb_ref, wout_ref, o_ref,
         hn_scr, u_scr, carry_scr, z_scr, c_scr, v_scr, y_scr) = refs
    T = hn_scr.shape[0]
    E = wout_ref.shape[0]
    g = pl.program_id(0)
    m = jnp.minimum(g, n_tiles - 1)
    slot = g % 2

    @pl.when(g == 0)
    def _():
        c_scr[...] = jnp.zeros(c_scr.shape, jnp.float32)
        z_scr[...] = jnp.zeros(z_scr.shape, jnp.float32)
        if natural_in:
            hp_scr[...] = jnp.zeros(hp_scr.shape, jnp.float32)

    @pl.when(m % tiles_per_seq == 0)
    def _():
        carry_scr[...] = jnp.zeros(carry_scr.shape, jnp.float32)

    if natural_in:
        h_main = hp_scr.at[slot]
        h_back = hp_scr.at[1 - slot]
        _interleave(h_main, hm_ref)
    else:
        h_main = hm_ref
        h_back = he_ref
    z_main = z_scr.at[slot]
    z_back = z_scr.at[1 - slot]
    _normalize_input(h_main, g_ref, hn_scr)

    half = T // 2
    for r0 in range(0, T, half):
        hn = hn_scr[r0:r0 + half, :]
        for c0 in range(0, E, COL_CHUNK):
            cols = slice(c0, c0 + COL_CHUNK)
            a = _dot(hn, win_ref[:, c0:c0 + COL_CHUNK])
            b = _dot(hn, win_ref[:, E + c0:E + c0 + COL_CHUNK])
            u_scr[CONV_HALO + r0:CONV_HALO + r0 + half, cols] = a * _sigmoid(b)
            z = _dot(hn, win_ref[:, 2 * E + c0:2 * E + c0 + COL_CHUNK])
            z_main[r0:r0 + half, cols] = z * _sigmoid(z)

    lng = lng_ref[...]
    lnb = lnb_ref[...]
    for h0 in range(0, T, OUT_ROWS):
        for r0 in range(h0, h0 + OUT_ROWS, ROW_CHUNK):
            rows = slice(r0, r0 + ROW_CHUNK)
            c = c_scr[rows, :]
            mu = jnp.mean(c, axis=-1, keepdims=True)
            cc = c - mu
            var = jnp.mean(cc * cc, axis=-1, keepdims=True)
            y = cc * lax.rsqrt(var + LN_EPS) * lng + lnb
            v_scr[rows, :] = (y * _sigmoid(y) * z_back[rows, :]).astype(jnp.bfloat16)
        y_scr[h0:h0 + OUT_ROWS, :] = _dot(v_scr[h0:h0 + OUT_ROWS, :], wout_ref[...])
    _residual_out(h_back, y_scr, o_ref, None, natural_out, False)

    _fill_history(u_scr, carry_scr, CONV_HALO, T, slice(0, E))
    _depthwise_conv(u_scr, dw_ref, dwb_ref, c_scr, T)


def _pool_layer_kernel(h_ref, g_ref, win_ref, wgrp_ref, bgrp_ref, scale_ref, wout_ref,
                       fg_ref, o_ref, hp_scr, hn_scr, u_scr, carry_scr, d_scr, v_scr,
                       y_scr, *, natural_in, natural_out, final_norm):
    T = hn_scr.shape[0]
    E = wout_ref.shape[0]
    gc = E // len(POOL_WINDOWS)

    @pl.when(pl.program_id(1) == 0)
    def _():
        carry_scr[...] = jnp.zeros(carry_scr.shape, jnp.float32)

    if natural_in:
        _interleave(hp_scr, h_ref)
        h_src = hp_scr
    else:
        h_src = h_ref
    _normalize_input(h_src, g_ref, hn_scr)

    t0 = pl.program_id(1) * T
    for g, w in enumerate(POOL_WINDOWS):
        cols = slice(g * gc, (g + 1) * gc)
        hn = hn_scr[...]
        u_scr[POOL_HALO:POOL_HALO + T, cols] = _dot(hn, win_ref[:, g * gc:(g + 1) * gc])
        z = _dot(hn, win_ref[:, E + g * gc:E + (g + 1) * gc])
        sz = z * _sigmoid(z)

        _fill_history(u_scr, carry_scr, POOL_HALO, T, cols)

        back = SUBLANES * (w - 1)
        for r0 in range(0, T, POOL_ROWS):
            p = u_scr[r0 + POOL_HALO - back:r0 + POOL_HALO + POOL_ROWS, cols]
            cur = p[back:, :]
            span = 1
            while span < w:
                shift = SUBLANES * span
                p = p[shift:, :] + p[:-shift, :]
                span *= 2
            short = min(max(SUBLANES * w - r0, 0), POOL_ROWS)
            if short < POOL_ROWS:
                d = p[short:, :] * (1.0 / w) - cur[short:, :]
                d_scr[r0 + short:r0 + POOL_ROWS, cols] = d.astype(jnp.bfloat16)
            if short:
                row = r0 + lax.broadcasted_iota(jnp.int32, (short, gc), 0)
                pos1 = t0 + (row % SUBLANES) * SEG + row // SUBLANES + 1
                ds = p[:short, :] / jnp.minimum(pos1, w).astype(jnp.float32) - cur[:short, :]
                d_scr[r0:r0 + short, cols] = ds.astype(jnp.bfloat16)

        yg = _dot(d_scr[:, cols], wgrp_ref[g])
        yg = (yg + bgrp_ref[:, cols]) * scale_ref[:, cols] * sz
        v_scr[:, cols] = yg.astype(jnp.bfloat16)

    half = T // 2
    for r0 in range(0, T, half):
        y_scr[r0:r0 + half, :] = _dot(v_scr[r0:r0 + half, :], wout_ref[...])
    _residual_out(h_src, y_scr, o_ref, fg_ref, natural_out, final_norm)


def _full_spec(shape):
    return pl.BlockSpec(shape, lambda b, s: (0,) * len(shape))


def _tile_spec(d):
    return pl.BlockSpec((None, TILE_T, d), lambda b, s: (b, s, 0))


def _compiler_params():
    return pltpu.CompilerParams(
        dimension_semantics=("arbitrary", "arbitrary"),
        vmem_limit_bytes=VMEM_LIMIT_BYTES,
    )


def _check_tiling(S, E):
    assert S % TILE_T == 0 and TILE_T % ROW_CHUNK == 0
    assert TILE_T % CONV_ROWS == 0 and TILE_T % POOL_ROWS == 0 and E % COL_CHUNK == 0
    assert SEG >= CONV_WIDTH - 1 and SEG >= max(POOL_WINDOWS) - 1


def _conv_layer(h, g, w_in, dw, dw_b, ln_g, ln_b, w_out, natural_in, natural_out):
    B, S, D = h.shape
    E = w_out.shape[0]
    _check_tiling(S, E)
    f32, bf16 = jnp.float32, jnp.bfloat16
    tiles_per_seq = S // TILE_T
    n_tiles = B * tiles_per_seq

    def front_tile(i):
        t = jnp.minimum(i, n_tiles - 1)
        return (t // tiles_per_seq, t % tiles_per_seq, 0)

    def back_tile(i):
        t = jnp.maximum(i - 1, 0)
        return (t // tiles_per_seq, t % tiles_per_seq, 0)

    def full(shape):
        return pl.BlockSpec(shape, lambda i: (0,) * len(shape))

    tile = (None, TILE_T, D)
    h_specs = [pl.BlockSpec(tile, front_tile)]
    h_args = [h]
    scratch = []
    if natural_in:
        scratch.append(pltpu.VMEM((2, TILE_T, D), f32))
    else:
        h_specs.append(pl.BlockSpec(tile, back_tile))
        h_args.append(h)
    scratch += [
        pltpu.VMEM((TILE_T, D), bf16),
        pltpu.VMEM((CONV_HALO + TILE_T, E), f32),
        pltpu.VMEM((CONV_HALO, E), f32),
        pltpu.VMEM((2, TILE_T, E), f32),
        pltpu.VMEM((TILE_T, E), f32),
        pltpu.VMEM((TILE_T, E), bf16),
        pltpu.VMEM((TILE_T, D), f32),
    ]
    return pl.pallas_call(
        functools.partial(_conv_layer_kernel, natural_in=natural_in, natural_out=natural_out,
                          tiles_per_seq=tiles_per_seq, n_tiles=n_tiles),
        grid=(n_tiles + 1,),
        in_specs=h_specs + [
            full((1, D)),
            full((D, 3 * E)),
            full((CONV_WIDTH, E)),
            full((1, E)),
            full((1, E)),
            full((1, E)),
            full((E, D)),
        ],
        out_specs=pl.BlockSpec(tile, back_tile),
        out_shape=jax.ShapeDtypeStruct((B, S, D), f32),
        scratch_shapes=scratch,
        compiler_params=pltpu.CompilerParams(
            dimension_semantics=("arbitrary",),
            vmem_limit_bytes=VMEM_LIMIT_BYTES,
        ),
        name="conv_layer_in" if natural_in else "conv_layer",
    )(*h_args, g.reshape(1, D), w_in.astype(bf16), dw, dw_b.reshape(1, E),
      ln_g.reshape(1, E), ln_b.reshape(1, E), w_out.astype(bf16))


def _pool_layer(h, g, w_in, w_grp, b_grp, scale, w_out, final_g, natural_in,
                natural_out, final_norm):
    B, S, D = h.shape
    E = w_out.shape[0]
    n_groups, gc, _ = w_grp.shape
    assert n_groups == len(POOL_WINDOWS)
    _check_tiling(S, E)
    f32, bf16 = jnp.float32, jnp.bfloat16
    return pl.pallas_call(
        functools.partial(_pool_layer_kernel, natural_in=natural_in,
                          natural_out=natural_out, final_norm=final_norm),
        grid=(B, S // TILE_T),
        in_specs=[
            _tile_spec(D),
            _full_spec((1, D)),
            _full_spec((D, 2 * E)),
            _full_spec((n_groups, gc, gc)),
            _full_spec((1, E)),
            _full_spec((1, E)),
            _full_spec((E, D)),
            _full_spec((1, D)),
        ],
        out_specs=_tile_spec(D),
        out_shape=jax.ShapeDtypeStruct((B, S, D), f32),
        scratch_shapes=[
            pltpu.VMEM((TILE_T if natural_in else SUBLANES, D), f32),
            pltpu.VMEM((TILE_T, D), bf16),
            pltpu.VMEM((POOL_HALO + TILE_T, E), f32),
            pltpu.VMEM((POOL_HALO, E), f32),
            pltpu.VMEM((TILE_T, E), bf16),
            pltpu.VMEM((TILE_T, E), bf16),
            pltpu.VMEM((TILE_T, D), f32),
        ],
        compiler_params=_compiler_params(),
        name="pool_layer_out" if natural_out else "pool_layer",
    )(h, g.reshape(1, D), w_in.astype(bf16), w_grp.astype(bf16),
      b_grp.reshape(1, E), scale.reshape(1, E), w_out.astype(bf16),
      final_g.reshape(1, D))


def kernel(x, norm_g, final_g, conv_w_in, conv_dw, conv_dw_b, conv_ln_g, conv_ln_b,
           conv_w_out, pool_w_in, pool_w_grp, pool_b_grp, pool_scale, pool_w_out):
    depth = norm_g.shape[0]
    h = x
    for i in range(depth):
        j = i // 2
        first, last = i == 0, i == depth - 1
        if i % 2 == 0:
            assert not last, "the final RMSNorm is fused into a pooling layer"
            h = _conv_layer(h, norm_g[i], conv_w_in[j], conv_dw[j], conv_dw_b[j],
                            conv_ln_g[j], conv_ln_b[j], conv_w_out[j],
                            natural_in=first, natural_out=last)
        else:
            h = _pool_layer(h, norm_g[i], pool_w_in[j], pool_w_grp[j], pool_b_grp[j],
                            pool_scale[j], pool_w_out[j], final_g,
                            natural_in=first, natural_out=last, final_norm=last)
    return h
```

```python
import functools

import jax
import jax.numpy as jnp
from jax import lax
from jax.experimental import pallas as pl
from jax.experimental.pallas import tpu as pltpu

RMS_EPS = 1e-6
LN_EPS = 1e-5
CONV_WIDTH = 31
POOL_WINDOWS = (2, 4, 8, 16)

SUBLANES = 8
LANES = 128

TILE_T = 1024
SEG = TILE_T // SUBLANES
CONV_HALO = SUBLANES * (CONV_WIDTH - 1)
POOL_HALO = SUBLANES * (max(POOL_WINDOWS) - 1)
ROW_CHUNK = 64
CONV_ROWS = 64
POOL_ROWS = 128
COL_CHUNK = 256
OUT_ROWS = 256
VMEM_LIMIT_BYTES = 60 * 1024 * 1024


def _sigmoid(x):
    return 1.0 / (1.0 + jnp.exp(-x))


def _rmsnorm_rows(x, g):
    ms = jnp.mean(x * x, axis=-1, keepdims=True)
    return x * lax.rsqrt(ms + RMS_EPS) * g


def _dot(a, b):
    return jnp.dot(a, b, preferred_element_type=jnp.float32)


def _interleave(dst_ref, src_ref):
    D = src_ref.shape[1]
    for jb in range(SEG // SUBLANES):
        x = jnp.stack([src_ref[i * SEG + jb * SUBLANES:i * SEG + (jb + 1) * SUBLANES, :]
                       for i in range(SUBLANES)], axis=0)
        y = jnp.transpose(x, (1, 0, 2)).reshape(SUBLANES * SUBLANES, D)
        dst_ref[jb * SUBLANES * SUBLANES:(jb + 1) * SUBLANES * SUBLANES, :] = y


def _deinterleave(dst_ref, src_ref):
    D = src_ref.shape[1]
    for jb in range(SEG // SUBLANES):
        y = src_ref[jb * SUBLANES * SUBLANES:(jb + 1) * SUBLANES * SUBLANES, :]
        x = jnp.transpose(y.reshape(SUBLANES, SUBLANES, D), (1, 0, 2))
        for i in range(SUBLANES):
            dst_ref[i * SEG + jb * SUBLANES:i * SEG + (jb + 1) * SUBLANES, :] = x[i]


def _normalize_input(h_ref, g_ref, hn_scr):
    g = g_ref[...]
    for r0 in range(0, hn_scr.shape[0], ROW_CHUNK):
        rows = slice(r0, r0 + ROW_CHUNK)
        hn_scr[rows, :] = _rmsnorm_rows(h_ref[rows, :], g).astype(jnp.bfloat16)


def _fill_history(u_scr, carry_scr, halo, T, cols):
    width = cols.stop - cols.start
    sub = lax.broadcasted_iota(jnp.int32, (SUBLANES, width), 0)
    for jj in range(halo // SUBLANES):
        rows = slice(jj * SUBLANES, (jj + 1) * SUBLANES)
        tail = u_scr[T + jj * SUBLANES:T + (jj + 1) * SUBLANES, cols]
        merged = jnp.where(sub == SUBLANES - 1, carry_scr[rows, cols], tail)
        u_scr[rows, cols] = pltpu.roll(merged, 1, axis=0)
        carry_scr[rows, cols] = tail


def _depthwise_conv(u_scr, dw_ref, dwb_ref, c_scr, T):
    n_out = CONV_ROWS // SUBLANES
    E = u_scr.shape[1]

    def lane_tile(l, carry):
        lanes = pl.ds(pl.multiple_of(l * LANES, LANES), LANES)
        w = dw_ref[:, lanes]
        wb = [jnp.broadcast_to(w[k:k + 1, :], (SUBLANES, LANES)) for k in range(CONV_WIDTH)]
        bias = jnp.broadcast_to(dwb_ref[:, lanes], (SUBLANES, LANES))
        for r0 in range(0, T, CONV_ROWS):
            acc = [bias] * n_out
            for m in range(n_out + CONV_WIDTH - 1):
                x = u_scr[r0 + m * SUBLANES:r0 + (m + 1) * SUBLANES, lanes]
                for o in range(n_out):
                    k = m - o
                    if 0 <= k < CONV_WIDTH:
                        acc[o] = acc[o] + x * wb[k]
            c_scr[r0:r0 + CONV_ROWS, lanes] = jnp.concatenate(acc, axis=0)
        return carry
    lax.fori_loop(0, E // LANES, lane_tile, 0)


def _residual_out(h_src, y_scr, o_ref, fg_ref, natural_out, final_norm):
    dst = y_scr if natural_out else o_ref
    for r0 in range(0, y_scr.shape[0], ROW_CHUNK):
        rows = slice(r0, r0 + ROW_CHUNK)
        out = h_src[rows, :] + y_scr[rows, :]
        if final_norm:
            out = _rmsnorm_rows(out, fg_ref[...])
        dst[rows, :] = out
    if natural_out:
        _deinterleave(o_ref, y_scr)


def _conv_layer_kernel(*refs, natural_in, natural_out, tiles_per_seq, n_tiles):
    if natural_in:
        (hm_ref, g_ref, win_ref, dw_ref, dwb_ref, lng_ref, lnb_ref, wout_ref, o_ref,
         hp_scr, hn_scr, u_scr, carry_scr, z_scr, c_scr, v_scr, y_scr) = refs
    else:
        (hm_ref, he_ref, g_ref, win_ref, dw_ref, dwb_ref, lng_ref, lnb_ref, wout_ref, o_ref,
         hn_scr, u_scr, carry_scr, z_scr, c_scr, v_scr, y_scr) = refs
    T = hn_scr.shape[0]
    E = wout_ref.shape[0]
    g = pl.program_id(0)
    m = jnp.minimum(g, n_tiles - 1)
    slot = g % 2

    @pl.when(g == 0)
    def _():
        c_scr[...] = jnp.zeros(c_scr.shape, jnp.float32)
        z_scr[...] = jnp.zeros(z_scr.shape, jnp.float32)
        if natural_in:
            hp_scr[...] = jnp.zeros(hp_scr.shape, jnp.float32)

    @pl.when(m % tiles_per_seq == 0)
    def _():
        carry_scr[...] = jnp.zeros(carry_scr.shape, jnp.float32)

    if natural_in:
        h_main = hp_scr.at[slot]
        h_back = hp_scr.at[1 - slot]
        _interleave(h_main, hm_ref)
    else:
        h_main = hm_ref
        h_back = he_ref
    z_main = z_scr.at[slot]
    z_back = z_scr.at[1 - slot]
    _normalize_input(h_main, g_ref, hn_scr)

    half = T // 2
    for r0 in range(0, T, half):
        hn = hn_scr[r0:r0 + half, :]
        for c0 in range(0, E, COL_CHUNK):
            cols = slice(c0, c0 + COL_CHUNK)
            a = _dot(hn, win_ref[:, c0:c0 + COL_CHUNK])
            b = _dot(hn, win_ref[:, E + c0:E + c0 + COL_CHUNK])
            u_scr[CONV_HALO + r0:CONV_HALO + r0 + half, cols] = a * _sigmoid(b)
            z = _dot(hn, win_ref[:, 2 * E + c0:2 * E + c0 + COL_CHUNK])
            z_main[r0:r0 + half, cols] = z * _sigmoid(z)

    lng = lng_ref[...]
    lnb = lnb_ref[...]
    for h0 in range(0, T, OUT_ROWS):
        for r0 in range(h0, h0 + OUT_ROWS, ROW_CHUNK):
            rows = slice(r0, r0 + ROW_CHUNK)
            c = c_scr[rows, :]
            mu = jnp.mean(c, axis=-1, keepdims=True)
            cc = c - mu
            var = jnp.mean(cc * cc, axis=-1, keepdims=True)
            y = cc * lax.rsqrt(var + LN_EPS) * lng + lnb
            v_scr[rows, :] = (y * _sigmoid(y) * z_back[rows, :]).astype(jnp.bfloat16)
        y_scr[h0:h0 + OUT_ROWS, :] = _dot(v_scr[h0:h0 + OUT_ROWS, :], wout_ref[...])
    _residual_out(h_back, y_scr, o_ref, None, natural_out, False)

    _fill_history(u_scr, carry_scr, CONV_HALO, T, slice(0, E))
    _depthwise_conv(u_scr, dw_ref, dwb_ref, c_scr, T)


def _pool_layer_kernel(h_ref, g_ref, win_ref, wgrp_ref, bgrp_ref, scale_ref, wout_ref,
                       fg_ref, o_ref, hp_scr, hn_scr, u_scr, carry_scr, d_scr, v_scr,
                       y_scr, *, natural_in, natural_out, final_norm):
    T = hn_scr.shape[0]
    E = wout_ref.shape[0]
    gc = E // len(POOL_WINDOWS)

    @pl.when(pl.program_id(1) == 0)
    def _():
        carry_scr[...] = jnp.zeros(carry_scr.shape, jnp.float32)

    if natural_in:
        _interleave(hp_scr, h_ref)
        h_src = hp_scr
    else:
        h_src = h_ref
    _normalize_input(h_src, g_ref, hn_scr)

    t0 = pl.program_id(1) * T
    for g, w in enumerate(POOL_WINDOWS):
        cols = slice(g * gc, (g + 1) * gc)
        hn = hn_scr[...]
        u_scr[POOL_HALO:POOL_HALO + T, cols] = _dot(hn, win_ref[:, g * gc:(g + 1) * gc])
        z = _dot(hn, win_ref[:, E + g * gc:E + (g + 1) * gc])
        sz = z * _sigmoid(z)

        _fill_history(u_scr, carry_scr, POOL_HALO, T, cols)

        back = SUBLANES * (w - 1)
        for r0 in range(0, T, POOL_ROWS):
            p = u_scr[r0 + POOL_HALO - back:r0 + POOL_HALO + POOL_ROWS, cols]
            cur = p[back:, :]
            span = 1
            while span < w:
                shift = SUBLANES * span
                p = p[shift:, :] + p[:-shift, :]
                span *= 2
            short = min(max(SUBLANES * w - r0, 0), POOL_ROWS)
            if short < POOL_ROWS:
                d = p[short:, :] * (1.0 / w) - cur[short:, :]
                d_scr[r0 + short:r0 + POOL_ROWS, cols] = d.astype(jnp.bfloat16)
            if short:
                row = r0 + lax.broadcasted_iota(jnp.int32, (short, gc), 0)
                pos1 = t0 + (row % SUBLANES) * SEG + row // SUBLANES + 1
                ds = p[:short, :] / jnp.minimum(pos1, w).astype(jnp.float32) - cur[:short, :]
                d_scr[r0:r0 + short, cols] = ds.astype(jnp.bfloat16)

        yg = _dot(d_scr[:, cols], wgrp_ref[g])
        yg = (yg + bgrp_ref[:, cols]) * scale_ref[:, cols] * sz
        v_scr[:, cols] = yg.astype(jnp.bfloat16)

    half = T // 2
    for r0 in range(0, T, half):
        y_scr[r0:r0 + half, :] = _dot(v_scr[r0:r0 + half, :], wout_ref[...])
    _residual_out(h_src, y_scr, o_ref, fg_ref, natural_out, final_norm)


def _full_spec(shape):
    return pl.BlockSpec(shape, lambda b, s: (0,) * len(shape))


def _tile_spec(d):
    return pl.BlockSpec((None, TILE_T, d), lambda b, s: (b, s, 0))


def _compiler_params():
    return pltpu.CompilerParams(
        dimension_semantics=("arbitrary", "arbitrary"),
        vmem_limit_bytes=VMEM_LIMIT_BYTES,
    )


def _check_tiling(S, E):
    assert S % TILE_T == 0 and TILE_T % ROW_CHUNK == 0
    assert TILE_T % CONV_ROWS == 0 and TILE_T % POOL_ROWS == 0 and E % COL_CHUNK == 0
    assert SEG >= CONV_WIDTH - 1 and SEG >= max(POOL_WINDOWS) - 1


def _conv_layer(h, g, w_in, dw, dw_b, ln_g, ln_b, w_out, natural_in, natural_out):
    B, S, D = h.shape
    E = w_out.shape[0]
    _check_tiling(S, E)
    f32, bf16 = jnp.float32, jnp.bfloat16
    tiles_per_seq = S // TILE_T
    n_tiles = B * tiles_per_seq

    def front_tile(i):
        t = jnp.minimum(i, n_tiles - 1)
        return (t // tiles_per_seq, t % tiles_per_seq, 0)

    def back_tile(i):
        t = jnp.maximum(i - 1, 0)
        return (t // tiles_per_seq, t % tiles_per_seq, 0)

    def full(shape):
        return pl.BlockSpec(shape, lambda i: (0,) * len(shape))

    tile = (None, TILE_T, D)
    h_specs = [pl.BlockSpec(tile, front_tile)]
    h_args = [h]
    scratch = []
    if natural_in:
        scratch.append(pltpu.VMEM((2, TILE_T, D), f32))
    else:
        h_specs.append(pl.BlockSpec(tile, back_tile))
        h_args.append(h)
    scratch += [
        pltpu.VMEM((TILE_T, D), bf16),
        pltpu.VMEM((CONV_HALO + TILE_T, E), f32),
        pltpu.VMEM((CONV_HALO, E), f32),
        pltpu.VMEM((2, TILE_T, E), f32),
        pltpu.VMEM((TILE_T, E), f32),
        pltpu.VMEM((TILE_T, E), bf16),
        pltpu.VMEM((TILE_T, D), f32),
    ]
    return pl.pallas_call(
        functools.partial(_conv_layer_kernel, natural_in=natural_in, natural_out=natural_out,
                          tiles_per_seq=tiles_per_seq, n_tiles=n_tiles),
        grid=(n_tiles + 1,),
        in_specs=h_specs + [
            full((1, D)),
            full((D, 3 * E)),
            full((CONV_WIDTH, E)),
            full((1, E)),
            full((1, E)),
            full((1, E)),
            full((E, D)),
        ],
        out_specs=pl.BlockSpec(tile, back_tile),
        out_shape=jax.ShapeDtypeStruct((B, S, D), f32),
        scratch_shapes=scratch,
        compiler_params=pltpu.CompilerParams(
            dimension_semantics=("arbitrary",),
            vmem_limit_bytes=VMEM_LIMIT_BYTES,
        ),
        name="conv_layer_in" if natural_in else "conv_layer",
    )(*h_args, g.reshape(1, D), w_in.astype(bf16), dw, dw_b.reshape(1, E),
      ln_g.reshape(1, E), ln_b.reshape(1, E), w_out.astype(bf16))


def _pool_layer(h, g, w_in, w_grp, b_grp, scale, w_out, final_g, natural_in,
                natural_out, final_norm):
    B, S, D = h.shape
    E = w_out.shape[0]
    n_groups, gc, _ = w_grp.shape
    assert n_groups == len(POOL_WINDOWS)
    _check_tiling(S, E)
    f32, bf16 = jnp.float32, jnp.bfloat16
    return pl.pallas_call(
        functools.partial(_pool_layer_kernel, natural_in=natural_in,
                          natural_out=natural_out, final_norm=final_norm),
        grid=(B, S // TILE_T),
        in_specs=[
            _tile_spec(D),
            _full_spec((1, D)),
            _full_spec((D, 2 * E)),
            _full_spec((n_groups, gc, gc)),
            _full_spec((1, E)),
            _full_spec((1, E)),
            _full_spec((E, D)),
            _full_spec((1, D)),
        ],
        out_specs=_tile_spec(D),
        out_shape=jax.ShapeDtypeStruct((B, S, D), f32),
        scratch_shapes=[
            pltpu.VMEM((TILE_T if natural_in else SUBLANES, D), f32),
            pltpu.VMEM((TILE_T, D), bf16),
            pltpu.VMEM((POOL_HALO + TILE_T, E), f32),
            pltpu.VMEM((POOL_HALO, E), f32),
            pltpu.VMEM((TILE_T, E), bf16),
            pltpu.VMEM((TILE_T, E), bf16),
            pltpu.VMEM((TILE_T, D), f32),
        ],
        compiler_params=_compiler_params(),
        name="pool_layer_out" if natural_out else "pool_layer",
    )(h, g.reshape(1, D), w_in.astype(bf16), w_grp.astype(bf16),
      b_grp.reshape(1, E), scale.reshape(1, E), w_out.astype(bf16),
      final_g.reshape(1, D))


def kernel(x, norm_g, final_g, conv_w_in, conv_dw, conv_dw_b, conv_ln_g, conv_ln_b,
           conv_w_out, pool_w_in, pool_w_grp, pool_b_grp, pool_scale, pool_w_out):
    depth = norm_g.shape[0]
    h = x
    for i in range(depth):
        j = i // 2
        first, last = i == 0, i == depth - 1
        if i % 2 == 0:
            assert not last, "the final RMSNorm is fused into a pooling layer"
            h = _conv_layer(h, norm_g[i], conv_w_in[j], conv_dw[j], conv_dw_b[j],
                            conv_ln_g[j], conv_ln_b[j], conv_w_out[j],
                            natural_in=first, natural_out=last)
        else:
            h = _pool_layer(h, norm_g[i], pool_w_in[j], pool_w_grp[j], pool_b_grp[j],
                            pool_scale[j], pool_w_out[j], final_g,
                            natural_in=first, natural_out=last, final_norm=last)
    return h
```

```python
import functools

import jax
import jax.numpy as jnp
from jax import lax
from jax.experimental import pallas as pl
from jax.experimental.pallas import tpu as pltpu

RMS_EPS = 1e-6
LN_EPS = 1e-5
CONV_WIDTH = 31
POOL_WINDOWS = (2, 4, 8, 16)

SUBLANES = 8
LANES = 128

TILE_T = 1024
SEG = TILE_T // SUBLANES
CONV_HALO = SUBLANES * (CONV_WIDTH - 1)
POOL_HALO = SUBLANES * (max(POOL_WINDOWS) - 1)
ROW_CHUNK = 64
CONV_ROWS = 64
POOL_ROWS = 128
COL_CHUNK = 256
OUT_ROWS = 256
VMEM_LIMIT_BYTES = 60 * 1024 * 1024


def _sigmoid(x):
    return 1.0 / (1.0 + jnp.exp(-x))


def _rmsnorm_rows(x, g):
    ms = jnp.mean(x * x, axis=-1, keepdims=True)
    return x * lax.rsqrt(ms + RMS_EPS) * g


def _dot(a, b):
    return jnp.dot(a, b, preferred_element_type=jnp.float32)


def _interleave(dst_ref, src_ref):
    D = src_ref.shape[1]
    for jb in range(SEG // SUBLANES):
        x = jnp.stack([src_ref[i * SEG + jb * SUBLANES:i * SEG + (jb + 1) * SUBLANES, :]
                       for i in range(SUBLANES)], axis=0)
        y = jnp.transpose(x, (1, 0, 2)).reshape(SUBLANES * SUBLANES, D)
        dst_ref[jb * SUBLANES * SUBLANES:(jb + 1) * SUBLANES * SUBLANES, :] = y


def _deinterleave(dst_ref, src_ref):
    D = src_ref.shape[1]
    for jb in range(SEG // SUBLANES):
        y = src_ref[jb * SUBLANES * SUBLANES:(jb + 1) * SUBLANES * SUBLANES, :]
        x = jnp.transpose(y.reshape(SUBLANES, SUBLANES, D), (1, 0, 2))
        for i in range(SUBLANES):
            dst_ref[i * SEG + jb * SUBLANES:i * SEG + (jb + 1) * SUBLANES, :] = x[i]


def _normalize_input(h_ref, g_ref, hn_scr):
    g = g_ref[...]
    for r0 in range(0, hn_scr.shape[0], ROW_CHUNK):
        rows = slice(r0, r0 + ROW_CHUNK)
        hn_scr[rows, :] = _rmsnorm_rows(h_ref[rows, :], g).astype(jnp.bfloat16)


def _fill_history(u_scr, carry_scr, halo, T, cols):
    width = cols.stop - cols.start
    sub = lax.broadcasted_iota(jnp.int32, (SUBLANES, width), 0)
    for jj in range(halo // SUBLANES):
        rows = slice(jj * SUBLANES, (jj + 1) * SUBLANES)
        tail = u_scr[T + jj * SUBLANES:T + (jj + 1) * SUBLANES, cols]
        merged = jnp.where(sub == SUBLANES - 1, carry_scr[rows, cols], tail)
        u_scr[rows, cols] = pltpu.roll(merged, 1, axis=0)
        carry_scr[rows, cols] = tail


def _depthwise_conv(u_scr, dw_ref, dwb_ref, c_scr, T):
    n_out = CONV_ROWS // SUBLANES
    E = u_scr.shape[1]

    def lane_tile(l, carry):
        lanes = pl.ds(pl.multiple_of(l * LANES, LANES), LANES)
        w = dw_ref[:, lanes]
        wb = [jnp.broadcast_to(w[k:k + 1, :], (SUBLANES, LANES)) for k in range(CONV_WIDTH)]
        bias = jnp.broadcast_to(dwb_ref[:, lanes], (SUBLANES, LANES))
        for r0 in range(0, T, CONV_ROWS):
            acc = [bias] * n_out
            for m in range(n_out + CONV_WIDTH - 1):
                x = u_scr[r0 + m * SUBLANES:r0 + (m + 1) * SUBLANES, lanes]
                for o in range(n_out):
                    k = m - o
                    if 0 <= k < CONV_WIDTH:
                        acc[o] = acc[o] + x * wb[k]
            c_scr[r0:r0 + CONV_ROWS, lanes] = jnp.concatenate(acc, axis=0)
        return carry
    lax.fori_loop(0, E // LANES, lane_tile, 0)


def _residual_out(h_src, y_scr, o_ref, fg_ref, natural_out, final_norm):
    dst = y_scr if natural_out else o_ref
    for r0 in range(0, y_scr.shape[0], ROW_CHUNK):
        rows = slice(r0, r0 + ROW_CHUNK)
        out = h_src[rows, :] + y_scr[rows, :]
        if final_norm:
            out = _rmsnorm_rows(out, fg_ref[...])
        dst[rows, :] = out
    if natural_out:
        _deinterleave(o_ref, y_scr)


def _conv_layer_kernel(*refs, natural_in, natural_out, tiles_per_seq, n_tiles):
    if natural_in:
        (hm_ref, g_ref, win_ref, dw_ref, dwb_ref, lng_ref, lnb_ref, wout_ref, o_ref,
         hp_scr, hn_scr, u_scr, carry_scr, z_scr, c_scr, v_scr, y_scr) = refs
    else:
        (hm_ref, he_ref, g_ref, win_ref, dw_ref, dwb_ref, lng_ref, lnb_ref, wout_ref, o_ref,
         hn_scr, u_scr, carry_scr, z_scr, c_scr, v_scr, y_scr) = refs
    T = hn_scr.shape[0]
    E = wout_ref.shape[0]
    g = pl.program_id(0)
    m = jnp.minimum(g, n_tiles - 1)
    slot = g % 2

    @pl.when(g == 0)
    def _():
        c_scr[...] = jnp.zeros(c_scr.shape, jnp.float32)
        z_scr[...] = jnp.zeros(z_scr.shape, jnp.float32)
        if natural_in:
            hp_scr[...] = jnp.zeros(hp_scr.shape, jnp.float32)

    @pl.when(m % tiles_per_seq == 0)
    def _():
        carry_scr[...] = jnp.zeros(carry_scr.shape, jnp.float32)

    if natural_in:
        h_main = hp_scr.at[slot]
        h_back = hp_scr.at[1 - slot]
        _interleave(h_main, hm_ref)
    else:
        h_main = hm_ref
        h_back = he_ref
    z_main = z_scr.at[slot]
    z_back = z_scr.at[1 - slot]
    _normalize_input(h_main, g_ref, hn_scr)

    half = T // 4
    for r0 in range(0, T, half):
        hn = hn_scr[r0:r0 + half, :]
        for c0 in range(0, E, COL_CHUNK):
            cols = slice(c0, c0 + COL_CHUNK)
            a = _dot(hn, win_ref[:, c0:c0 + COL_CHUNK])
            b = _dot(hn, win_ref[:, E + c0:E + c0 + COL_CHUNK])
            u_scr[CONV_HALO + r0:CONV_HALO + r0 + half, cols] = a * _sigmoid(b)
            z = _dot(hn, win_ref[:, 2 * E + c0:2 * E + c0 + COL_CHUNK])
            z_main[r0:r0 + half, cols] = z * _sigmoid(z)

    lng = lng_ref[...]
    lnb = lnb_ref[...]
    for h0 in range(0, T, OUT_ROWS):
        for r0 in range(h0, h0 + OUT_ROWS, ROW_CHUNK):
            rows = slice(r0, r0 + ROW_CHUNK)
            c = c_scr[rows, :]
            mu = jnp.mean(c, axis=-1, keepdims=True)
            cc = c - mu
            var = jnp.mean(cc * cc, axis=-1, keepdims=True)
            y = cc * lax.rsqrt(var + LN_EPS) * lng + lnb
            v_scr[rows, :] = (y * _sigmoid(y) * z_back[rows, :]).astype(jnp.bfloat16)
        y_scr[h0:h0 + OUT_ROWS, :] = _dot(v_scr[h0:h0 + OUT_ROWS, :], wout_ref[...])
    _residual_out(h_back, y_scr, o_ref, None, natural_out, False)

    _fill_history(u_scr, carry_scr, CONV_HALO, T, slice(0, E))
    _depthwise_conv(u_scr, dw_ref, dwb_ref, c_scr, T)


def _pool_layer_kernel(h_ref, g_ref, win_ref, wgrp_ref, bgrp_ref, scale_ref, wout_ref,
                       fg_ref, o_ref, hp_scr, hn_scr, u_scr, carry_scr, d_scr, v_scr,
                       y_scr, *, natural_in, natural_out, final_norm):
    T = hn_scr.shape[0]
    E = wout_ref.shape[0]
    gc = E // len(POOL_WINDOWS)

    @pl.when(pl.program_id(1) == 0)
    def _():
        carry_scr[...] = jnp.zeros(carry_scr.shape, jnp.float32)

    if natural_in:
        _interleave(hp_scr, h_ref)
        h_src = hp_scr
    else:
        h_src = h_ref
    _normalize_input(h_src, g_ref, hn_scr)

    t0 = pl.program_id(1) * T
    for g, w in enumerate(POOL_WINDOWS):
        cols = slice(g * gc, (g + 1) * gc)
        hn = hn_scr[...]
        u_scr[POOL_HALO:POOL_HALO + T, cols] = _dot(hn, win_ref[:, g * gc:(g + 1) * gc])
        z = _dot(hn, win_ref[:, E + g * gc:E + (g + 1) * gc])
        sz = z * _sigmoid(z)

        _fill_history(u_scr, carry_scr, POOL_HALO, T, cols)

        back = SUBLANES * (w - 1)
        for r0 in range(0, T, POOL_ROWS):
            p = u_scr[r0 + POOL_HALO - back:r0 + POOL_HALO + POOL_ROWS, cols]
            cur = p[back:, :]
            span = 1
            while span < w:
                shift = SUBLANES * span
                p = p[shift:, :] + p[:-shift, :]
                span *= 2
            short = min(max(SUBLANES * w - r0, 0), POOL_ROWS)
            if short < POOL_ROWS:
                d = p[short:, :] * (1.0 / w) - cur[short:, :]
                d_scr[r0 + short:r0 + POOL_ROWS, cols] = d.astype(jnp.bfloat16)
            if short:
                row = r0 + lax.broadcasted_iota(jnp.int32, (short, gc), 0)
                pos1 = t0 + (row % SUBLANES) * SEG + row // SUBLANES + 1
                ds = p[:short, :] / jnp.minimum(pos1, w).astype(jnp.float32) - cur[:short, :]
                d_scr[r0:r0 + short, cols] = ds.astype(jnp.bfloat16)

        yg = _dot(d_scr[:, cols], wgrp_ref[g])
        yg = (yg + bgrp_ref[:, cols]) * scale_ref[:, cols] * sz
        v_scr[:, cols] = yg.astype(jnp.bfloat16)

    half = T // 2
    for r0 in range(0, T, half):
        y_scr[r0:r0 + half, :] = _dot(v_scr[r0:r0 + half, :], wout_ref[...])
    _residual_out(h_src, y_scr, o_ref, fg_ref, natural_out, final_norm)


def _full_spec(shape):
    return pl.BlockSpec(shape, lambda b, s: (0,) * len(shape))


def _tile_spec(d):
    return pl.BlockSpec((None, TILE_T, d), lambda b, s: (b, s, 0))


def _compiler_params():
    return pltpu.CompilerParams(
        dimension_semantics=("arbitrary", "arbitrary"),
        vmem_limit_bytes=VMEM_LIMIT_BYTES,
    )


def _check_tiling(S, E):
    assert S % TILE_T == 0 and TILE_T % ROW_CHUNK == 0
    assert TILE_T % CONV_ROWS == 0 and TILE_T % POOL_ROWS == 0 and E % COL_CHUNK == 0
    assert SEG >= CONV_WIDTH - 1 and SEG >= max(POOL_WINDOWS) - 1


def _conv_layer(h, g, w_in, dw, dw_b, ln_g, ln_b, w_out, natural_in, natural_out):
    B, S, D = h.shape
    E = w_out.shape[0]
    _check_tiling(S, E)
    f32, bf16 = jnp.float32, jnp.bfloat16
    tiles_per_seq = S // TILE_T
    n_tiles = B * tiles_per_seq

    def front_tile(i):
        t = jnp.minimum(i, n_tiles - 1)
        return (t // tiles_per_seq, t % tiles_per_seq, 0)

    def back_tile(i):
        t = jnp.maximum(i - 1, 0)
        return (t // tiles_per_seq, t % tiles_per_seq, 0)

    def full(shape):
        return pl.BlockSpec(shape, lambda i: (0,) * len(shape))

    tile = (None, TILE_T, D)
    h_specs = [pl.BlockSpec(tile, front_tile)]
    h_args = [h]
    scratch = []
    if natural_in:
        scratch.append(pltpu.VMEM((2, TILE_T, D), f32))
    else:
        h_specs.append(pl.BlockSpec(tile, back_tile))
        h_args.append(h)
    scratch += [
        pltpu.VMEM((TILE_T, D), bf16),
        pltpu.VMEM((CONV_HALO + TILE_T, E), f32),
        pltpu.VMEM((CONV_HALO, E), f32),
        pltpu.VMEM((2, TILE_T, E), f32),
        pltpu.VMEM((TILE_T, E), f32),
        pltpu.VMEM((TILE_T, E), bf16),
        pltpu.VMEM((TILE_T, D), f32),
    ]
    return pl.pallas_call(
        functools.partial(_conv_layer_kernel, natural_in=natural_in, natural_out=natural_out,
                          tiles_per_seq=tiles_per_seq, n_tiles=n_tiles),
        grid=(n_tiles + 1,),
        in_specs=h_specs + [
            full((1, D)),
            full((D, 3 * E)),
            full((CONV_WIDTH, E)),
            full((1, E)),
            full((1, E)),
            full((1, E)),
            full((E, D)),
        ],
        out_specs=pl.BlockSpec(tile, back_tile),
        out_shape=jax.ShapeDtypeStruct((B, S, D), f32),
        scratch_shapes=scratch,
        compiler_params=pltpu.CompilerParams(
            dimension_semantics=("arbitrary",),
            vmem_limit_bytes=VMEM_LIMIT_BYTES,
        ),
        name="conv_layer_in" if natural_in else "conv_layer",
    )(*h_args, g.reshape(1, D), w_in.astype(bf16), dw, dw_b.reshape(1, E),
      ln_g.reshape(1, E), ln_b.reshape(1, E), w_out.astype(bf16))


def _pool_layer(h, g, w_in, w_grp, b_grp, scale, w_out, final_g, natural_in,
                natural_out, final_norm):
    B, S, D = h.shape
    E = w_out.shape[0]
    n_groups, gc, _ = w_grp.shape
    assert n_groups == len(POOL_WINDOWS)
    _check_tiling(S, E)
    f32, bf16 = jnp.float32, jnp.bfloat16
    return pl.pallas_call(
        functools.partial(_pool_layer_kernel, natural_in=natural_in,
                          natural_out=natural_out, final_norm=final_norm),
        grid=(B, S // TILE_T),
        in_specs=[
            _tile_spec(D),
            _full_spec((1, D)),
            _full_spec((D, 2 * E)),
            _full_spec((n_groups, gc, gc)),
            _full_spec((1, E)),
            _full_spec((1, E)),
            _full_spec((E, D)),
            _full_spec((1, D)),
        ],
        out_specs=_tile_spec(D),
        out_shape=jax.ShapeDtypeStruct((B, S, D), f32),
        scratch_shapes=[
            pltpu.VMEM((TILE_T if natural_in else SUBLANES, D), f32),
            pltpu.VMEM((TILE_T, D), bf16),
            pltpu.VMEM((POOL_HALO + TILE_T, E), f32),
            pltpu.VMEM((POOL_HALO, E), f32),
            pltpu.VMEM((TILE_T, E), bf16),
            pltpu.VMEM((TILE_T, E), bf16),
            pltpu.VMEM((TILE_T, D), f32),
        ],
        compiler_params=_compiler_params(),
        name="pool_layer_out" if natural_out else "pool_layer",
    )(h, g.reshape(1, D), w_in.astype(bf16), w_grp.astype(bf16),
      b_grp.reshape(1, E), scale.reshape(1, E), w_out.astype(bf16),
      final_g.reshape(1, D))


def kernel(x, norm_g, final_g, conv_w_in, conv_dw, conv_dw_b, conv_ln_g, conv_ln_b,
           conv_w_out, pool_w_in, pool_w_grp, pool_b_grp, pool_scale, pool_w_out):
    depth = norm_g.shape[0]
    h = x
    for i in range(depth):
        j = i // 2
        first, last = i == 0, i == depth - 1
        if i % 2 == 0:
            assert not last, "the final RMSNorm is fused into a pooling layer"
            h = _conv_layer(h, norm_g[i], conv_w_in[j], conv_dw[j], conv_dw_b[j],
                            conv_ln_g[j], conv_ln_b[j], conv_w_out[j],
                            natural_in=first, natural_out=last)
        else:
            h = _pool_layer(h, norm_g[i], pool_w_in[j], pool_w_grp[j], pool_b_grp[j],
                            pool_scale[j], pool_w_out[j], final_g,
                            natural_in=first, natural_out=last, final_norm=last)
    return h
```

```python
import functools

import jax
import jax.numpy as jnp
from jax import lax
from jax.experimental import pallas as pl
from jax.experimental.pallas import tpu as pltpu

RMS_EPS = 1e-6
LN_EPS = 1e-5
CONV_WIDTH = 31
POOL_WINDOWS = (2, 4, 8, 16)

SUBLANES = 8
LANES = 128

TILE_T = 1024
SEG = TILE_T // SUBLANES
CONV_HALO = SUBLANES * (CONV_WIDTH - 1)
POOL_HALO = SUBLANES * (max(POOL_WINDOWS) - 1)
ROW_CHUNK = 64
CONV_ROWS = 64
POOL_ROWS = 128
COL_CHUNK = 256
OUT_ROWS = 256
VMEM_LIMIT_BYTES = 60 * 1024 * 1024


def _sigmoid(x):
    return 1.0 / (1.0 + jnp.exp(-x))


def _rmsnorm_rows(x, g):
    ms = jnp.mean(x * x, axis=-1, keepdims=True)
    return x * lax.rsqrt(ms + RMS_EPS) * g


def _dot(a, b):
    return jnp.dot(a, b, preferred_element_type=jnp.float32)


def _interleave(dst_ref, src_ref):
    D = src_ref.shape[1]
    for jb in range(SEG // SUBLANES):
        x = jnp.stack([src_ref[i * SEG + jb * SUBLANES:i * SEG + (jb + 1) * SUBLANES, :]
                       for i in range(SUBLANES)], axis=0)
        y = jnp.transpose(x, (1, 0, 2)).reshape(SUBLANES * SUBLANES, D)
        dst_ref[jb * SUBLANES * SUBLANES:(jb + 1) * SUBLANES * SUBLANES, :] = y


def _deinterleave(dst_ref, src_ref):
    D = src_ref.shape[1]
    for jb in range(SEG // SUBLANES):
        y = src_ref[jb * SUBLANES * SUBLANES:(jb + 1) * SUBLANES * SUBLANES, :]
        x = jnp.transpose(y.reshape(SUBLANES, SUBLANES, D), (1, 0, 2))
        for i in range(SUBLANES):
            dst_ref[i * SEG + jb * SUBLANES:i * SEG + (jb + 1) * SUBLANES, :] = x[i]


def _normalize_input(h_ref, g_ref, hn_scr):
    g = g_ref[...]
    for r0 in range(0, hn_scr.shape[0], ROW_CHUNK):
        rows = slice(r0, r0 + ROW_CHUNK)
        hn_scr[rows, :] = _rmsnorm_rows(h_ref[rows, :], g).astype(jnp.bfloat16)


def _fill_history(u_scr, carry_scr, halo, T, cols):
    width = cols.stop - cols.start
    sub = lax.broadcasted_iota(jnp.int32, (SUBLANES, width), 0)
    for jj in range(halo // SUBLANES):
        rows = slice(jj * SUBLANES, (jj + 1) * SUBLANES)
        tail = u_scr[T + jj * SUBLANES:T + (jj + 1) * SUBLANES, cols]
        merged = jnp.where(sub == SUBLANES - 1, carry_scr[rows, cols], tail)
        u_scr[rows, cols] = pltpu.roll(merged, 1, axis=0)
        carry_scr[rows, cols] = tail


def _depthwise_conv(u_scr, dw_ref, dwb_ref, c_scr, T):
    n_out = CONV_ROWS // SUBLANES
    E = u_scr.shape[1]

    def lane_tile(l, carry):
        lanes = pl.ds(pl.multiple_of(l * LANES, LANES), LANES)
        w = dw_ref[:, lanes]
        wb = [jnp.broadcast_to(w[k:k + 1, :], (SUBLANES, LANES)) for k in range(CONV_WIDTH)]
        bias = jnp.broadcast_to(dwb_ref[:, lanes], (SUBLANES, LANES))
        for r0 in range(0, T, CONV_ROWS):
            acc = [bias] * n_out
            for m in range(n_out + CONV_WIDTH - 1):
                x = u_scr[r0 + m * SUBLANES:r0 + (m + 1) * SUBLANES, lanes]
                for o in range(n_out):
                    k = m - o
                    if 0 <= k < CONV_WIDTH:
                        acc[o] = acc[o] + x * wb[k]
            c_scr[r0:r0 + CONV_ROWS, lanes] = jnp.concatenate(acc, axis=0)
        return carry
    lax.fori_loop(0, E // LANES, lane_tile, 0)


def _residual_out(h_src, y_scr, o_ref, fg_ref, natural_out, final_norm):
    dst = y_scr if natural_out else o_ref
    for r0 in range(0, y_scr.shape[0], ROW_CHUNK):
        rows = slice(r0, r0 + ROW_CHUNK)
        out = h_src[rows, :] + y_scr[rows, :]
        if final_norm:
            out = _rmsnorm_rows(out, fg_ref[...])
        dst[rows, :] = out
    if natural_out:
        _deinterleave(o_ref, y_scr)


def _conv_layer_kernel(*refs, natural_in, natural_out, tiles_per_seq, n_tiles):
    if natural_in:
        (hm_ref, g_ref, win_ref, dw_ref, dwb_ref, lng_ref, lnb_ref, wout_ref, o_ref,
         hp_scr, hn_scr, u_scr, carry_scr, z_scr, c_scr, v_scr, y_scr) = refs
    else:
        (hm_ref, he_ref, g_ref, win_ref, dw_ref, dwb_ref, lng_ref, lnb_ref, wout_ref, o_ref,
         hn_scr, u_scr, carry_scr, z_scr, c_scr, v_scr, y_scr) = refs
    T = hn_scr.shape[0]
    E = wout_ref.shape[0]
    g = pl.program_id(0)
    m = jnp.minimum(g, n_tiles - 1)
    slot = g % 2

    @pl.when(g == 0)
    def _():
        c_scr[...] = jnp.zeros(c_scr.shape, jnp.float32)
        z_scr[...] = jnp.zeros(z_scr.shape, jnp.float32)
        if natural_in:
            hp_scr[...] = jnp.zeros(hp_scr.shape, jnp.float32)

    @pl.when(m % tiles_per_seq == 0)
    def _():
        carry_scr[...] = jnp.zeros(carry_scr.shape, jnp.float32)

    if natural_in:
        h_main = hp_scr.at[slot]
        h_back = hp_scr.at[1 - slot]
        _interleave(h_main, hm_ref)
    else:
        h_main = hm_ref
        h_back = he_ref
    z_main = z_scr.at[slot]
    z_back = z_scr.at[1 - slot]
    _normalize_input(h_main, g_ref, hn_scr)

    half = T // 4
    for r0 in range(0, T, half):
        hn = hn_scr[r0:r0 + half, :]
        for c0 in range(0, E, COL_CHUNK):
            cols = slice(c0, c0 + COL_CHUNK)
            a = _dot(hn, win_ref[:, c0:c0 + COL_CHUNK])
            b = _dot(hn, win_ref[:, E + c0:E + c0 + COL_CHUNK])
            u_scr[CONV_HALO + r0:CONV_HALO + r0 + half, cols] = a * _sigmoid(b)
            z = _dot(hn, win_ref[:, 2 * E + c0:2 * E + c0 + COL_CHUNK])
            z_main[r0:r0 + half, cols] = z * _sigmoid(z)

    lng = lng_ref[...]
    lnb = lnb_ref[...]
    for h0 in range(0, T, OUT_ROWS):
        for r0 in range(h0, h0 + OUT_ROWS, ROW_CHUNK):
            rows = slice(r0, r0 + ROW_CHUNK)
            c = c_scr[rows, :]
            mu = jnp.mean(c, axis=-1, keepdims=True)
            cc = c - mu
            var = jnp.mean(cc * cc, axis=-1, keepdims=True)
            y = cc * lax.rsqrt(var + LN_EPS) * lng + lnb
            v_scr[rows, :] = (y * _sigmoid(y) * z_back[rows, :]).astype(jnp.bfloat16)
        y_scr[h0:h0 + OUT_ROWS, :] = _dot(v_scr[h0:h0 + OUT_ROWS, :], wout_ref[...])
    _residual_out(h_back, y_scr, o_ref, None, natural_out, False)

    _fill_history(u_scr, carry_scr, CONV_HALO, T, slice(0, E))
    _depthwise_conv(u_scr, dw_ref, dwb_ref, c_scr, T)


def _pool_layer_kernel(hm_ref, he_ref, g_ref, win_ref, wgrp_ref, bgrp_ref, scale_ref,
                       wout_ref, fg_ref, o_ref, hn_scr, u_scr, carry_scr, d_scr, v_scr,
                       y_scr, *, natural_out, final_norm, tiles_per_seq, n_tiles):
    T = hn_scr.shape[0]
    E = wout_ref.shape[0]
    gc = E // len(POOL_WINDOWS)
    i = pl.program_id(0)
    seq_tile = jnp.minimum(i, n_tiles - 1) % tiles_per_seq
    slot = i % 2
    y_main = y_scr.at[slot]
    y_back = y_scr.at[1 - slot]

    @pl.when(i == 0)
    def _():
        y_scr[...] = jnp.zeros(y_scr.shape, jnp.float32)

    @pl.when(seq_tile == 0)
    def _():
        carry_scr[...] = jnp.zeros(carry_scr.shape, jnp.float32)

    _residual_out(he_ref, y_back, o_ref, fg_ref, natural_out, final_norm)
    _normalize_input(hm_ref, g_ref, hn_scr)

    t0 = seq_tile * T
    for g, w in enumerate(POOL_WINDOWS):
        cols = slice(g * gc, (g + 1) * gc)
        hn = hn_scr[...]
        u_scr[POOL_HALO:POOL_HALO + T, cols] = _dot(hn, win_ref[:, g * gc:(g + 1) * gc])
        z = _dot(hn, win_ref[:, E + g * gc:E + (g + 1) * gc])
        sz = z * _sigmoid(z)

        _fill_history(u_scr, carry_scr, POOL_HALO, T, cols)

        back = SUBLANES * (w - 1)
        for r0 in range(0, T, POOL_ROWS):
            p = u_scr[r0 + POOL_HALO - back:r0 + POOL_HALO + POOL_ROWS, cols]
            cur = p[back:, :]
            span = 1
            while span < w:
                shift = SUBLANES * span
                p = p[shift:, :] + p[:-shift, :]
                span *= 2
            short = min(max(SUBLANES * w - r0, 0), POOL_ROWS)
            if short < POOL_ROWS:
                d = p[short:, :] * (1.0 / w) - cur[short:, :]
                d_scr[r0 + short:r0 + POOL_ROWS, cols] = d.astype(jnp.bfloat16)
            if short:
                row = r0 + lax.broadcasted_iota(jnp.int32, (short, gc), 0)
                pos1 = t0 + (row % SUBLANES) * SEG + row // SUBLANES + 1
                ds = p[:short, :] / jnp.minimum(pos1, w).astype(jnp.float32) - cur[:short, :]
                d_scr[r0:r0 + short, cols] = ds.astype(jnp.bfloat16)

        yg = _dot(d_scr[:, cols], wgrp_ref[g])
        yg = (yg + bgrp_ref[:, cols]) * scale_ref[:, cols] * sz
        v_scr[:, cols] = yg.astype(jnp.bfloat16)

    for r0 in range(0, T, OUT_ROWS):
        y_main[r0:r0 + OUT_ROWS, :] = _dot(v_scr[r0:r0 + OUT_ROWS, :], wout_ref[...])


def _check_tiling(S, E):
    assert S % TILE_T == 0 and TILE_T % ROW_CHUNK == 0
    assert TILE_T % CONV_ROWS == 0 and TILE_T % POOL_ROWS == 0 and E % COL_CHUNK == 0
    assert SEG >= CONV_WIDTH - 1 and SEG >= max(POOL_WINDOWS) - 1


def _conv_layer(h, g, w_in, dw, dw_b, ln_g, ln_b, w_out, natural_in, natural_out):
    B, S, D = h.shape
    E = w_out.shape[0]
    _check_tiling(S, E)
    f32, bf16 = jnp.float32, jnp.bfloat16
    tiles_per_seq = S // TILE_T
    n_tiles = B * tiles_per_seq

    def front_tile(i):
        t = jnp.minimum(i, n_tiles - 1)
        return (t // tiles_per_seq, t % tiles_per_seq, 0)

    def back_tile(i):
        t = jnp.maximum(i - 1, 0)
        return (t // tiles_per_seq, t % tiles_per_seq, 0)

    def full(shape):
        return pl.BlockSpec(shape, lambda i: (0,) * len(shape))

    tile = (None, TILE_T, D)
    h_specs = [pl.BlockSpec(tile, front_tile)]
    h_args = [h]
    scratch = []
    if natural_in:
        scratch.append(pltpu.VMEM((2, TILE_T, D), f32))
    else:
        h_specs.append(pl.BlockSpec(tile, back_tile))
        h_args.append(h)
    scratch += [
        pltpu.VMEM((TILE_T, D), bf16),
        pltpu.VMEM((CONV_HALO + TILE_T, E), f32),
        pltpu.VMEM((CONV_HALO, E), f32),
        pltpu.VMEM((2, TILE_T, E), f32),
        pltpu.VMEM((TILE_T, E), f32),
        pltpu.VMEM((TILE_T, E), bf16),
        pltpu.VMEM((TILE_T, D), f32),
    ]
    return pl.pallas_call(
        functools.partial(_conv_layer_kernel, natural_in=natural_in, natural_out=natural_out,
                          tiles_per_seq=tiles_per_seq, n_tiles=n_tiles),
        grid=(n_tiles + 1,),
        in_specs=h_specs + [
            full((1, D)),
            full((D, 3 * E)),
            full((CONV_WIDTH, E)),
            full((1, E)),
            full((1, E)),
            full((1, E)),
            full((E, D)),
        ],
        out_specs=pl.BlockSpec(tile, back_tile),
        out_shape=jax.ShapeDtypeStruct((B, S, D), f32),
        scratch_shapes=scratch,
        compiler_params=pltpu.CompilerParams(
            dimension_semantics=("arbitrary",),
            vmem_limit_bytes=VMEM_LIMIT_BYTES,
        ),
        name="conv_layer_in" if natural_in else "conv_layer",
    )(*h_args, g.reshape(1, D), w_in.astype(bf16), dw, dw_b.reshape(1, E),
      ln_g.reshape(1, E), ln_b.reshape(1, E), w_out.astype(bf16))


def _pool_layer(h, g, w_in, w_grp, b_grp, scale, w_out, final_g, natural_out, final_norm):
    B, S, D = h.shape
    E = w_out.shape[0]
    n_groups, gc, _ = w_grp.shape
    assert n_groups == len(POOL_WINDOWS)
    _check_tiling(S, E)
    f32, bf16 = jnp.float32, jnp.bfloat16
    tiles_per_seq = S // TILE_T
    n_tiles = B * tiles_per_seq

    def front_tile(i):
        t = jnp.minimum(i, n_tiles - 1)
        return (t // tiles_per_seq, t % tiles_per_seq, 0)

    def back_tile(i):
        t = jnp.maximum(i - 1, 0)
        return (t // tiles_per_seq, t % tiles_per_seq, 0)

    def full(shape):
        return pl.BlockSpec(shape, lambda i: (0,) * len(shape))

    tile = (None, TILE_T, D)
    return pl.pallas_call(
        functools.partial(_pool_layer_kernel, natural_out=natural_out, final_norm=final_norm,
                          tiles_per_seq=tiles_per_seq, n_tiles=n_tiles),
        grid=(n_tiles + 1,),
        in_specs=[
            pl.BlockSpec(tile, front_tile),
            pl.BlockSpec(tile, back_tile),
            full((1, D)),
            full((D, 2 * E)),
            full((n_groups, gc, gc)),
            full((1, E)),
            full((1, E)),
            full((E, D)),
            full((1, D)),
        ],
        out_specs=pl.BlockSpec(tile, back_tile),
        out_shape=jax.ShapeDtypeStruct((B, S, D), f32),
        scratch_shapes=[
            pltpu.VMEM((TILE_T, D), bf16),
            pltpu.VMEM((POOL_HALO + TILE_T, E), f32),
            pltpu.VMEM((POOL_HALO, E), f32),
            pltpu.VMEM((TILE_T, E), bf16),
            pltpu.VMEM((TILE_T, E), bf16),
            pltpu.VMEM((2, TILE_T, D), f32),
        ],
        compiler_params=pltpu.CompilerParams(
            dimension_semantics=("arbitrary",),
            vmem_limit_bytes=VMEM_LIMIT_BYTES,
        ),
        name="pool_layer_out" if natural_out else "pool_layer",
    )(h, h, g.reshape(1, D), w_in.astype(bf16), w_grp.astype(bf16),
      b_grp.reshape(1, E), scale.reshape(1, E), w_out.astype(bf16),
      final_g.reshape(1, D))


def kernel(x, norm_g, final_g, conv_w_in, conv_dw, conv_dw_b, conv_ln_g, conv_ln_b,
           conv_w_out, pool_w_in, pool_w_grp, pool_b_grp, pool_scale, pool_w_out):
    depth = norm_g.shape[0]
    h = x
    for i in range(depth):
        j = i // 2
        first, last = i == 0, i == depth - 1
        if i % 2 == 0:
            assert not last, "the final RMSNorm is fused into a pooling layer"
            h = _conv_layer(h, norm_g[i], conv_w_in[j], conv_dw[j], conv_dw_b[j],
                            conv_ln_g[j], conv_ln_b[j], conv_w_out[j],
                            natural_in=first, natural_out=last)
        else:
            assert not first, "the input is interleaved by a conv layer"
            h = _pool_layer(h, norm_g[i], pool_w_in[j], pool_w_grp[j], pool_b_grp[j],
                            pool_scale[j], pool_w_out[j], final_g,
                            natural_out=last, final_norm=last)
    return h
```
